```python
import math
import jax, jax.numpy as jnp
from jax import lax
import numpy as np

D_MODEL = 2048
BATCH = 4
SEQ = 2048
DEPTH = 4
DEC_BATCH = 128
DEC_SEQ = 8
PAST_LEN = 8192
PAGE_SIZE = 128

N_AB_LAYERS = (DEPTH + 1) // 2
N_C_LAYERS = DEPTH // 2
H_A = 8
Q_LORA = 512
KV_LORA = 512
NOPE_D = 128
ROPE_D = 64
V_D = 128
ROPE_THETA = 10000.0
H_B = 8
KV_B = 2
HD_B = 128
FORGET_BIAS_INIT = 3.0
H_C = 32
KV_C = 4
HD_C = 64
WINDOW = 128
NUM_BUCKETS = 32
MAX_DISTANCE = 128
D_FF = 4096
FFN_RES = 0.5
Q_BLOCK = 128
EPS = 1e-6
NEG_INF = -1e30
MLA_SCALE = (NOPE_D + ROPE_D) ** -0.5
FOX_SCALE = HD_B ** -0.5
C_SCALE = HD_C ** -0.5
G_B = H_B // KV_B
G_C = H_C // KV_C
MAX_EXACT = NUM_BUCKETS // 2

kernel_name = 'hybrid_mla_fox_swa_macaron_step'


def _rmsnorm(x, g):
    x32 = x.astype(jnp.float32)
    y = x32 * lax.rsqrt(jnp.mean(x32 * x32, axis=-1, keepdims=True) + EPS)
    return (y * g.astype(jnp.float32)).astype(x.dtype)


def _half_ffn(h, g, w_gate, w_up, w_down):
    n = _rmsnorm(h, g)
    return h + FFN_RES * ((jax.nn.silu(n @ w_gate) * (n @ w_up)) @ w_down)


def _rope(x, pos):
    half = x.shape[-1] // 2
    inv = ROPE_THETA ** (-jnp.arange(half, dtype=jnp.float32) / half)
    ang = pos.astype(jnp.float32)[:, None] * inv[None, :]
    ang = ang.reshape((pos.shape[0],) + (1,) * (x.ndim - 3) + (half,))
    cos, sin = jnp.cos(ang), jnp.sin(ang)
    x32 = x.astype(jnp.float32)
    x1, x2 = x32[..., :half], x32[..., half:]
    return jnp.concatenate([x1 * cos - x2 * sin, x1 * sin + x2 * cos], axis=-1).astype(x.dtype)


def _probs(logits, mask, sink=None):
    logits = jnp.where(mask, logits, NEG_INF)
    if sink is None:
        return jax.nn.softmax(logits, axis=-1)
    m = jnp.maximum(jnp.max(logits, axis=-1, keepdims=True), sink)
    e = jnp.exp(logits - m)
    return e / (jnp.sum(e, axis=-1, keepdims=True) + jnp.exp(sink - m))


def _t5_bucket(dist):
    n = jnp.maximum(dist, 0)
    nf = jnp.maximum(n, 1).astype(jnp.float32)
    large = MAX_EXACT + (jnp.log(nf / MAX_EXACT) / math.log(MAX_DISTANCE / MAX_EXACT)
                         * (NUM_BUCKETS - MAX_EXACT)).astype(jnp.int32)
    return jnp.where(n < MAX_EXACT, n, jnp.minimum(large, NUM_BUCKETS - 1))


def _ab_project(n, pos, prm):
    w_in, g_q_lat, w_q_up, g_kv_lat, _, _, g_qn, g_qr, _, g_kr, g_fq, g_fk, b_f, _ = prm
    B, T, _ = n.shape
    sizes = (Q_LORA, KV_LORA, ROPE_D, H_B * HD_B, KV_B * HD_B, KV_B * HD_B, H_B)
    idx = np.cumsum(sizes)[:-1].tolist()
    q_lat, c_kv, k_r, fq, fk, fv, fg = jnp.split(n @ w_in, idx, axis=-1)
    q = (_rmsnorm(q_lat, g_q_lat) @ w_q_up).reshape(B, T, H_A, NOPE_D + ROPE_D)
    q_nope = _rmsnorm(q[..., :NOPE_D], g_qn)
    q_rope = _rope(_rmsnorm(q[..., NOPE_D:], g_qr), pos)
    c_kv = _rmsnorm(c_kv, g_kv_lat)
    k_rope = _rope(_rmsnorm(k_r, g_kr), pos)
    fq = _rmsnorm(fq.reshape(B, T, KV_B, G_B, HD_B), g_fq)
    fk = _rmsnorm(fk.reshape(B, T, KV_B, HD_B), g_fk)
    fv = fv.reshape(B, T, KV_B, HD_B)
    logf = jax.nn.log_sigmoid(fg.astype(jnp.float32) + b_f.astype(jnp.float32))
    return q_nope, q_rope, c_kv, k_rope, fq, fk, fv, logf


def _ab_prompt(n, prm):
    w_uk, w_uv, g_kn, w_out = prm[4], prm[5], prm[8], prm[13]
    B, S, _ = n.shape
    nb = S // Q_BLOCK
    pos = jnp.arange(S, dtype=jnp.int32)
    q_nope, q_rope, c_kv, k_rope, fq, fk, fv, logf = _ab_project(n, pos, prm)
    k_nope = _rmsnorm((c_kv @ w_uk).reshape(B, S, H_A, NOPE_D), g_kn)
    v_a = (c_kv @ w_uv).reshape(B, S, H_A, V_D)
    F = jnp.cumsum(logf, axis=1)
    F_k = jnp.swapaxes(F, 1, 2)
    kpos = jnp.arange(S)

    def to_blocks(a):
        return jnp.moveaxis(a.reshape((B, nb, Q_BLOCK) + a.shape[2:]), 1, 0)

    def block(args):
        i, qn, qr, qf, F_q = args
        qpos = i * Q_BLOCK + jnp.arange(Q_BLOCK)
        mask = kpos[None, :] <= qpos[:, None]
        s_a = (jnp.einsum('bthd,bshd->bhts', qn, k_nope, preferred_element_type=jnp.float32)
               + jnp.einsum('bthd,bsd->bhts', qr, k_rope, preferred_element_type=jnp.float32)) * MLA_SCALE
        pr_a = _probs(s_a, mask)
        o_a = jnp.einsum('bhts,bshd->bthd', pr_a.astype(v_a.dtype), v_a)
        bias = (jnp.swapaxes(F_q, 1, 2)[..., None] - F_k[:, :, None, :]).reshape(B, KV_B, G_B, Q_BLOCK, S)
        s_b = jnp.einsum('btkgd,bskd->bkgts', qf, fk, preferred_element_type=jnp.float32) * FOX_SCALE + bias
        pr_b = _probs(s_b, mask)
        o_b = jnp.einsum('bkgts,bskd->btkgd', pr_b.astype(fv.dtype), fv)
        return o_a.reshape(B, Q_BLOCK, H_A * V_D), o_b.reshape(B, Q_BLOCK, H_B * HD_B)

    o_a, o_b = lax.map(block, (jnp.arange(nb), to_blocks(q_nope), to_blocks(q_rope),
                               to_blocks(fq), to_blocks(F)))
    o = jnp.moveaxis(jnp.concatenate([o_a, o_b], axis=-1), 0, 1).reshape(B, S, H_A * V_D + H_B * HD_B)
    return o @ w_out, (c_kv, k_rope, fk, fv, logf)


def _ab_sample(n, j, cache_ckv, cache_krope, cache_fk, cache_fv, cache_logf, page_table, prm):
    w_uk, w_uv, g_kn, w_out = prm[4], prm[5], prm[8], prm[13]
    DB, T, _ = n.shape
    past = page_table.shape[1] * PAGE_SIZE
    S = past + T
    pos = past + jnp.arange(T, dtype=jnp.int32)
    q_nope, q_rope, c_kv, k_rope, fq, fk, fv, logf = _ab_project(n, pos, prm)
    mask = jnp.arange(S)[None, :] <= pos[:, None]
    w_uv_h = w_uv.reshape(KV_LORA, H_A, V_D)

    def rows(cache, pt):
        r = cache[j, pt]
        return r.reshape((past,) + r.shape[2:])

    def seq(args):
        pt, qn, qr, c_new, kr_new, qf, k_new, v_new, lf_new = args
        c_all = jnp.concatenate([rows(cache_ckv, pt), c_new], axis=0)
        kr_all = jnp.concatenate([rows(cache_krope, pt), kr_new], axis=0)
        k_nope = _rmsnorm((c_all @ w_uk).reshape(S, H_A, NOPE_D), g_kn)
        s_a = (jnp.einsum('thd,shd->hts', qn, k_nope, preferred_element_type=jnp.float32)
               + jnp.einsum('thd,sd->hts', qr, kr_all, preferred_element_type=jnp.float32)) * MLA_SCALE
        pr_a = _probs(s_a, mask)
        o_lat = jnp.einsum('hts,sc->thc', pr_a.astype(c_all.dtype), c_all)
        o_a = jnp.einsum('thc,chd->thd', o_lat, w_uv_h).reshape(T, H_A * V_D)
        k_all = jnp.concatenate([rows(cache_fk, pt), k_new], axis=0)
        v_all = jnp.concatenate([rows(cache_fv, pt), v_new], axis=0)
        lf_all = jnp.concatenate([rows(cache_logf, pt).astype(jnp.float32), lf_new], axis=0)
        F = jnp.cumsum(lf_all, axis=0)
        bias = (F[past:].T[:, :, None] - F.T[:, None, :]).reshape(KV_B, G_B, T, S)
        s_b = jnp.einsum('tkgd,skd->kgts', qf, k_all, preferred_element_type=jnp.float32) * FOX_SCALE + bias
        pr_b = _probs(s_b, mask)
        o_b = jnp.einsum('kgts,skd->tkgd', pr_b.astype(v_all.dtype), v_all).reshape(T, H_B * HD_B)
        return o_a, o_b

    o_a, o_b = lax.map(seq, (page_table, q_nope, q_rope, c_kv, k_rope, fq, fk, fv, logf))
    return jnp.concatenate([o_a, o_b], axis=-1) @ w_out, (c_kv, k_rope, fk, fv, logf)


def _c_project(n, w_in, g_q, g_k):
    B, T, _ = n.shape
    q, k, v = jnp.split(n @ w_in, [H_C * HD_C, (H_C + KV_C) * HD_C], axis=-1)
    return (_rmsnorm(q.reshape(B, T, KV_C, G_C, HD_C), g_q),
            _rmsnorm(k.reshape(B, T, KV_C, HD_C), g_k),
            v.reshape(B, T, KV_C, HD_C))


def _window_bias(q_loc, k_loc, rel_bias):
    dist = q_loc[:, None] - k_loc[None, :]
    bias = jnp.moveaxis(rel_bias[_t5_bucket(dist)].astype(jnp.float32), -1, 0)
    bias = bias.reshape(KV_C, G_C, q_loc.shape[0], k_loc.shape[0])
    return bias, (dist >= 0) & (dist < WINDOW)


def _c_prompt(n, w_in, g_q, g_k, sinks, w_out, rel_bias):
    B, S, _ = n.shape
    nb = S // WINDOW
    q, k, v = _c_project(n, w_in, g_q, g_k)

    def band(a):
        a = a.reshape(B, nb, WINDOW, KV_C, HD_C)
        prev = jnp.pad(a, ((0, 0), (1, 0), (0, 0), (0, 0), (0, 0)))[:, :-1]
        return jnp.concatenate([prev, a], axis=2)

    k_loc = jnp.arange(2 * WINDOW)
    bias, win = _window_bias(WINDOW + jnp.arange(WINDOW), k_loc, rel_bias)
    valid = (jnp.arange(nb)[:, None] > 0) | (k_loc[None, :] >= WINDOW)
    mask = win[None] & valid[:, None, :]
    s = jnp.einsum('bnqkgd,bnskd->bnkgqs', q.reshape(B, nb, WINDOW, KV_C, G_C, HD_C), band(k),
                   preferred_element_type=jnp.float32) * C_SCALE + bias
    pr = _probs(s, mask[:, None, None], sinks.astype(jnp.float32).reshape(KV_C, G_C, 1, 1))
    o = jnp.einsum('bnkgqs,bnskd->bnqkgd', pr.astype(v.dtype), band(v)).reshape(B, S, H_C * HD_C)
    return o @ w_out, (k[:, -WINDOW:], v[:, -WINDOW:])


def _c_sample(n, buf_k, buf_v, w_in, g_q, g_k, sinks, w_out, rel_bias):
    DB, T, _ = n.shape
    wb = buf_k.shape[1]
    q, k, v = _c_project(n, w_in, g_q, g_k)
    k_all = jnp.concatenate([buf_k, k], axis=1)
    v_all = jnp.concatenate([buf_v, v], axis=1)
    bias, mask = _window_bias(wb + jnp.arange(T), jnp.arange(wb + T), rel_bias)
    s = jnp.einsum('btkgd,bskd->bkgts', q, k_all, preferred_element_type=jnp.float32) * C_SCALE + bias
    pr = _probs(s, mask, sinks.astype(jnp.float32).reshape(KV_C, G_C, 1, 1))
    o = jnp.einsum('bkgts,bskd->btkgd', pr.astype(v_all.dtype), v_all).reshape(DB, T, H_C * HD_C)
    return o @ w_out, (k_all[:, -WINDOW:], v_all[:, -WINDOW:])


def setup_inputs(seed: int = 0) -> dict:
    key = jax.random.key(seed)
    ks = iter(jax.random.split(key, 48))
    f32 = jnp.float32
    n_pages = PAST_LEN // PAGE_SIZE
    n_used = DEC_BATCH * n_pages
    n_pool = n_used + n_used // 4
    win_rows = min(WINDOW, PAST_LEN)
    in_ab = Q_LORA + KV_LORA + ROPE_D + H_B * HD_B + 2 * KV_B * HD_B + H_B
    in_c = (H_C + 2 * KV_C) * HD_C
    out_ab = H_A * V_D + H_B * HD_B
    LA, LC = N_AB_LAYERS, N_C_LAYERS

    def rn(shape):
        return jax.random.normal(next(ks), shape, f32)

    def w(shape, fan_in):
        return rn(shape) * fan_in ** -0.5

    def gain(shape):
        return 1.0 + 0.05 * rn(shape)

    return {
        'x_prompt': rn((BATCH, SEQ, D_MODEL)),
        'x_sample': rn((DEC_BATCH, DEC_SEQ, D_MODEL)),
        'cache_mla_ckv': rn((LA, n_pool, PAGE_SIZE, KV_LORA)),
        'cache_mla_krope': rn((LA, n_pool, PAGE_SIZE, ROPE_D)),
        'cache_fox_k': rn((LA, n_pool, PAGE_SIZE, KV_B, HD_B)),
        'cache_fox_v': rn((LA, n_pool, PAGE_SIZE, KV_B, HD_B)),
        'cache_fox_logf': jax.nn.log_sigmoid(FORGET_BIAS_INIT + rn((LA, n_pool, PAGE_SIZE, H_B))),
        'state_swa_k': rn((LC, DEC_BATCH, win_rows, KV_C, HD_C)),
        'state_swa_v': rn((LC, DEC_BATCH, win_rows, KV_C, HD_C)),
        'page_table': jax.random.permutation(next(ks), n_pool)[:n_used].reshape(DEC_BATCH, n_pages).astype(jnp.int32),
        'ffn1_g': gain((DEPTH, D_MODEL)),
        'ffn1_w_gate': w((DEPTH, D_MODEL, D_FF), D_MODEL),
        'ffn1_w_up': w((DEPTH, D_MODEL, D_FF), D_MODEL),
        'ffn1_w_down': w((DEPTH, D_FF, D_MODEL), D_FF),
        'mix_g': gain((DEPTH, D_MODEL)),
        'ffn2_g': gain((DEPTH, D_MODEL)),
        'ffn2_w_gate': w((DEPTH, D_MODEL, D_FF), D_MODEL),
        'ffn2_w_up': w((DEPTH, D_MODEL, D_FF), D_MODEL),
        'ffn2_w_down': w((DEPTH, D_FF, D_MODEL), D_FF),
        'ab_w_in': w((LA, D_MODEL, in_ab), D_MODEL),
        'mla_g_q_lat': gain((LA, Q_LORA)),
        'mla_w_q_up': w((LA, Q_LORA, H_A * (NOPE_D + ROPE_D)), Q_LORA),
        'mla_g_kv_lat': gain((LA, KV_LORA)),
        'mla_w_uk': w((LA, KV_LORA, H_A * NOPE_D), KV_LORA),
        'mla_w_uv': w((LA, KV_LORA, H_A * V_D), KV_LORA),
        'mla_g_qn': gain((LA, NOPE_D)),
        'mla_g_qr': gain((LA, ROPE_D)),
        'mla_g_kn': gain((LA, NOPE_D)),
        'mla_g_kr': gain((LA, ROPE_D)),
        'fox_g_q': gain((LA, HD_B)),
        'fox_g_k': gain((LA, HD_B)),
        'fox_b_f': FORGET_BIAS_INIT + 0.5 * rn((LA, H_B)),
        'ab_w_out': w((LA, out_ab, D_MODEL), out_ab),
        'swa_w_in': w((LC, D_MODEL, in_c), D_MODEL),
        'swa_g_q': gain((LC, HD_C)),
        'swa_g_k': gain((LC, HD_C)),
        'swa_sinks': 0.5 * rn((LC, H_C)),
        'swa_w_out': w((LC, H_C * HD_C, D_MODEL), H_C * HD_C),
        'rel_bias': 0.5 * rn((NUM_BUCKETS, H_C)),
    }


def reference(x_prompt, x_sample, cache_mla_ckv, cache_mla_krope, cache_fox_k, cache_fox_v,
              cache_fox_logf, state_swa_k, state_swa_v, page_table,
              ffn1_g, ffn1_w_gate, ffn1_w_up, ffn1_w_down, mix_g,
              ffn2_g, ffn2_w_gate, ffn2_w_up, ffn2_w_down,
              ab_w_in, mla_g_q_lat, mla_w_q_up, mla_g_kv_lat, mla_w_uk, mla_w_uv,
              mla_g_qn, mla_g_qr, mla_g_kn, mla_g_kr, fox_g_q, fox_g_k, fox_b_f, ab_w_out,
              swa_w_in, swa_g_q, swa_g_k, swa_sinks, swa_w_out, rel_bias):
    xp, xs = x_prompt, x_sample
    ab_p, ab_s, c_p, c_s = [], [], [], []
    for l in range(DEPTH):
        xp = _half_ffn(xp, ffn1_g[l], ffn1_w_gate[l], ffn1_w_up[l], ffn1_w_down[l])
        xs = _half_ffn(xs, ffn1_g[l], ffn1_w_gate[l], ffn1_w_up[l], ffn1_w_down[l])
        hp = _rmsnorm(xp, mix_g[l])
        hs = _rmsnorm(xs, mix_g[l])
        j = l // 2
        if l % 2 == 0:
            prm = (ab_w_in[j], mla_g_q_lat[j], mla_w_q_up[j], mla_g_kv_lat[j], mla_w_uk[j], mla_w_uv[j],
                   mla_g_qn[j], mla_g_qr[j], mla_g_kn[j], mla_g_kr[j], fox_g_q[j], fox_g_k[j],
                   fox_b_f[j], ab_w_out[j])
            mp, st_p = _ab_prompt(hp, prm)
            ms, st_s = _ab_sample(hs, j, cache_mla_ckv, cache_mla_krope, cache_fox_k, cache_fox_v,
                                  cache_fox_logf, page_table, prm)
            ab_p.append(st_p)
            ab_s.append(st_s)
        else:
            mp, st_p = _c_prompt(hp, swa_w_in[j], swa_g_q[j], swa_g_k[j], swa_sinks[j], swa_w_out[j], rel_bias)
            ms, st_s = _c_sample(hs, state_swa_k[j], state_swa_v[j], swa_w_in[j], swa_g_q[j], swa_g_k[j],
                                 swa_sinks[j], swa_w_out[j], rel_bias)
            c_p.append(st_p)
            c_s.append(st_s)
        xp = xp + mp
        xs = xs + ms
        xp = _half_ffn(xp, ffn2_g[l], ffn2_w_gate[l], ffn2_w_up[l], ffn2_w_down[l])
        xs = _half_ffn(xs, ffn2_g[l], ffn2_w_gate[l], ffn2_w_up[l], ffn2_w_down[l])

    def stack(states, i):
        return jnp.stack([st[i] for st in states], axis=0)

    return (xp, xs,
            stack(ab_p, 0), stack(ab_p, 1), stack(ab_p, 2), stack(ab_p, 3), stack(ab_p, 4),
            stack(c_p, 0), stack(c_p, 1),
            stack(ab_s, 0), stack(ab_s, 1), stack(ab_s, 2), stack(ab_s, 3), stack(ab_s, 4),
            stack(c_s, 0), stack(c_s, 1))
```

```python
import functools
import math

import numpy as np
import jax
import jax.numpy as jnp
from jax import lax
from jax.experimental import pallas as pl
from jax.experimental.pallas import tpu as pltpu

f32, bf16, i32 = jnp.float32, jnp.bfloat16, jnp.int32

D_MODEL = 2048
BATCH = 4
SEQ = 2048
DEPTH = 4
DEC_BATCH = 128
DEC_SEQ = 8
PAST_LEN = 8192
PAGE_SIZE = 128
N_PAGES = PAST_LEN // PAGE_SIZE
H_A = 8
Q_LORA = 512
KV_LORA = 512
NOPE_D = 128
ROPE_D = 64
V_D = 128
ROPE_THETA = 10000.0
H_B = 8
KV_B = 2
HD_B = 128
H_C = 32
KV_C = 4
HD_C = 64
WINDOW = 128
NUM_BUCKETS = 32
MAX_DISTANCE = 128
D_FF = 4096
FFN_RES = 0.5
EPS = 1e-6
NEG_INF = -1e30
MLA_SCALE = (NOPE_D + ROPE_D) ** -0.5
FOX_SCALE = HD_B ** -0.5
C_SCALE = HD_C ** -0.5
G_B = H_B // KV_B
G_C = H_C // KV_C
MAX_EXACT = NUM_BUCKETS // 2

MP = BATCH * SEQ
MS = DEC_BATCH * DEC_SEQ
MT = MP + MS
IN_AB_PAD = 2688
QH = 256

LANES = 128
MXU_DIM = 256
VMEM_LIMIT = 56 * 1024 * 1024

NT = (((1,), (1,)), ((), ()))


def _cparams(sem):
    return pltpu.CompilerParams(dimension_semantics=sem, vmem_limit_bytes=VMEM_LIMIT)


def _rms(x, g, eps=EPS):
    return (x * lax.rsqrt(jnp.mean(x * x, axis=-1, keepdims=True) + eps)) * g


def _split3(x):
    hi = x.astype(bf16)
    r1 = x - hi.astype(f32)
    mid = r1.astype(bf16)
    lo = (r1 - mid.astype(f32)).astype(bf16)
    return hi, mid, lo


FFN_TM = 512
FFN_TF = 512


def _ffn_body(x_ref, g_ref, wg_ref, wu_ref, wd_ref, o_ref, n_ref):
    @pl.when(pl.program_id(1) == 0)
    def _():
        x = x_ref[...]
        n_ref[...] = _rms(x, g_ref[...]).astype(bf16)
        o_ref[...] = x

    n = n_ref[...]
    gate = jnp.dot(n, wg_ref[...], preferred_element_type=f32)
    up = jnp.dot(n, wu_ref[...], preferred_element_type=f32)
    h = (gate / (1.0 + jnp.exp(-gate))) * up * FFN_RES
    o_ref[...] += jnp.dot(h.astype(bf16), wd_ref[...], preferred_element_type=f32)


def _ffn(x, g, wg, wu, wd):
    m = x.shape[0]
    return pl.pallas_call(
        _ffn_body,
        grid=(m // FFN_TM, D_FF // FFN_TF),
        in_specs=[
            pl.BlockSpec((FFN_TM, D_MODEL), lambda i, f: (i, 0)),
            pl.BlockSpec((1, D_MODEL), lambda i, f: (0, 0)),
            pl.BlockSpec((D_MODEL, FFN_TF), lambda i, f: (0, f)),
            pl.BlockSpec((D_MODEL, FFN_TF), lambda i, f: (0, f)),
            pl.BlockSpec((FFN_TF, D_MODEL), lambda i, f: (f, 0)),
        ],
        out_specs=pl.BlockSpec((FFN_TM, D_MODEL), lambda i, f: (i, 0)),
        out_shape=jax.ShapeDtypeStruct((m, D_MODEL), f32),
        scratch_shapes=[pltpu.VMEM((FFN_TM, D_MODEL), bf16)],
        compiler_params=_cparams(("parallel", "arbitrary")),
        name="ffn_half",
    )(x, g.reshape(1, D_MODEL), wg, wu, wd)


OUT_TM = 512


def _outproj_body(x_ref, o_ref, w_ref, y_ref):
    y_ref[...] = x_ref[...] + jnp.dot(o_ref[...], w_ref[...], preferred_element_type=f32)


def _outproj(x, o, w):
    m = x.shape[0]
    return pl.pallas_call(
        _outproj_body,
        grid=(m // OUT_TM,),
        in_specs=[
            pl.BlockSpec((OUT_TM, D_MODEL), lambda i: (i, 0)),
            pl.BlockSpec((OUT_TM, D_MODEL), lambda i: (i, 0)),
            pl.BlockSpec((D_MODEL, D_MODEL), lambda i: (0, 0)),
        ],
        out_specs=pl.BlockSpec((OUT_TM, D_MODEL), lambda i: (i, 0)),
        out_shape=jax.ShapeDtypeStruct((m, D_MODEL), f32),
        compiler_params=_cparams(("parallel",)),
        name="out_proj",
    )(x, o, w)


AB_TM = 256


def _half_masks(shape):
    lane = lax.broadcasted_iota(i32, shape, 1)
    return lane < 64, (lane % 64) < 32


def _norm_rope_tile(t, g128, cosv, sinv, lo, first):
    ss = t * t
    s_lo = jnp.sum(jnp.where(lo, ss, 0.0), axis=-1, keepdims=True)
    s_hi = jnp.sum(jnp.where(lo, 0.0, ss), axis=-1, keepdims=True)
    r = jnp.where(lo, lax.rsqrt(s_lo / 64.0 + EPS), lax.rsqrt(s_hi / 64.0 + EPS))
    y = (t * r) * g128
    other = jnp.where(first, pltpu.roll(y, 96, 1), pltpu.roll(y, 32, 1))
    return y * cosv + other * sinv


def _abproj_body(x_ref, mg_ref, win_ref, gql_ref, wqu_ref, gkv_ref, gqn_ref, gqr_ref, gkr_ref,
                 gfq_ref, gfk_ref, bf_ref, cos_ref, sin_ref,
                 qmla_ref, ckv_ref, krope_ref, kr128_ref, fq_ref, fk_ref, fv_ref, logf_ref):
    n = _rms(x_ref[...], mg_ref[...]).astype(bf16)
    y = jnp.dot(n, win_ref[...], preferred_element_type=f32)
    cosv, sinv = cos_ref[...], sin_ref[...]
    lo, first = _half_masks(cosv.shape)

    ql = _rms(y[:, 0:Q_LORA], gql_ref[...]).astype(bf16)
    q = jnp.dot(ql, wqu_ref[...], preferred_element_type=f32)
    for h in range(H_A):
        t = _rms(q[:, h * NOPE_D:(h + 1) * NOPE_D], gqn_ref[...]) * MLA_SCALE
        qmla_ref[:, h * QH:h * QH + NOPE_D] = t.astype(bf16)
    base = H_A * NOPE_D
    for p in range(H_A // 2):
        t = _norm_rope_tile(q[:, base + p * LANES:base + (p + 1) * LANES], gqr_ref[...],
                            cosv, sinv, lo, first) * MLA_SCALE
        even = jnp.where(lo, t, 0.0)
        odd = jnp.where(lo, pltpu.roll(t, 64, 1), 0.0)
        qmla_ref[:, (2 * p) * QH + NOPE_D:(2 * p + 1) * QH] = even.astype(bf16)
        qmla_ref[:, (2 * p + 1) * QH + NOPE_D:(2 * p + 2) * QH] = odd.astype(bf16)

    ckv_ref[...] = _rms(y[:, 512:1024], gkv_ref[...])

    for h in range(H_B):
        t = _rms(y[:, 1024 + h * HD_B:1024 + (h + 1) * HD_B], gfq_ref[...]) * FOX_SCALE
        fq_ref[:, h * HD_B:(h + 1) * HD_B] = t.astype(bf16)
    for k in range(KV_B):
        fk_ref[:, k * HD_B:(k + 1) * HD_B] = _rms(y[:, 2048 + k * HD_B:2048 + (k + 1) * HD_B], gfk_ref[...])
    fv_ref[...] = y[:, 2304:2560]

    t = y[:, 2560:2688]
    kr = _norm_rope_tile(t, gkr_ref[...], cosv, sinv, lo, first)
    krope_ref[...] = kr[:, 0:ROPE_D]
    kr128_ref[...] = jnp.where(lo, kr, 0.0).astype(bf16)
    z = t + bf_ref[...]
    ls = jnp.minimum(z, 0.0) - jnp.log(1.0 + jnp.exp(-jnp.abs(z)))
    logf_ref[...] = ls[:, 64:64 + H_B]


def _abproj(x, mg, win, gql, wqu, gkv, gqn, gqr128, gkr128, gfq, gfk, bf128, cos128, sin128):
    m = x.shape[0]
    row = lambda w: pl.BlockSpec((AB_TM, w), lambda i: (i, 0))
    full = lambda a: pl.BlockSpec(a.shape, lambda i: (0,) * a.ndim)
    ins = [x, mg, win, gql, wqu, gkv, gqn, gqr128, gkr128, gfq, gfk, bf128, cos128, sin128]
    in_specs = [row(D_MODEL)] + [full(a) for a in ins[1:12]] + [row(LANES), row(LANES)]
    outs = [(H_A * QH, bf16), (KV_LORA, f32), (ROPE_D, f32), (LANES, bf16), (H_B * HD_B, bf16),
            (KV_B * HD_B, f32), (KV_B * HD_B, f32), (H_B, f32)]
    return pl.pallas_call(
        _abproj_body,
        grid=(m // AB_TM,),
        in_specs=in_specs,
        out_specs=[row(w) for w, _ in outs],
        out_shape=[jax.ShapeDtypeStruct((m, w), dt) for w, dt in outs],
        compiler_params=_cparams(("parallel",)),
        name="ab_proj",
    )(*ins)


KV_TM = 512


def _kvup_body(ckv_ref, kr128_ref, wuk_ref, wuv_ref, gkn_ref, kmla_ref, va_ref):
    c = ckv_ref[...].astype(bf16)
    kk = jnp.dot(c, wuk_ref[...], preferred_element_type=f32)
    kr = kr128_ref[...]
    for h in range(H_A):
        kmla_ref[:, h * QH:h * QH + NOPE_D] = _rms(kk[:, h * NOPE_D:(h + 1) * NOPE_D], gkn_ref[...]).astype(bf16)
        kmla_ref[:, h * QH + NOPE_D:(h + 1) * QH] = kr
    va_ref[...] = jnp.dot(c, wuv_ref[...], preferred_element_type=f32).astype(bf16)


def _kvup(ckv, kr128, wuk, wuv, gkn):
    m = MP
    row = lambda w: pl.BlockSpec((KV_TM, w), lambda i: (i, 0))
    full = lambda a: pl.BlockSpec(a.shape, lambda i: (0,) * a.ndim)
    return pl.pallas_call(
        _kvup_body,
        grid=(m // KV_TM,),
        in_specs=[row(KV_LORA), row(LANES), full(wuk), full(wuv), full(gkn)],
        out_specs=[row(H_A * QH), row(H_A * V_D)],
        out_shape=[jax.ShapeDtypeStruct((m, H_A * QH), bf16), jax.ShapeDtypeStruct((m, H_A * V_D), bf16)],
        compiler_params=_cparams(("parallel",)),
        name="mla_kv_up",
    )(ckv, kr128, wuk, wuv, gkn)


CS_CHUNK = MXU_DIM


def _tri_ones(n):
    r = lax.broadcasted_iota(i32, (n, n), 0)
    c = lax.broadcasted_iota(i32, (n, n), 1)
    return jnp.where(r <= c, 1.0, 0.0).astype(bf16)


def _cumsum_chunks(parts_of, n_chunks, write, carry):
    tri = _tri_ones(CS_CHUNK)
    for ch in range(n_chunks):
        cs = carry
        for p in parts_of(ch):
            cs = cs + jnp.dot(p, tri, preferred_element_type=f32)
        write(ch, cs)
        carry = cs[:, CS_CHUNK - 1:CS_CHUNK]
    return carry


def _cumsum_body(x_ref, o_ref):
    def parts_of(ch):
        return _split3(x_ref[:, ch * CS_CHUNK:(ch + 1) * CS_CHUNK])

    def write(ch, cs):
        o_ref[:, ch * CS_CHUNK:(ch + 1) * CS_CHUNK] = cs

    _cumsum_chunks(parts_of, SEQ // CS_CHUNK, write, jnp.zeros((H_B, 1), f32))


def _cumsum_prompt(lft):
    return pl.pallas_call(
        _cumsum_body,
        grid=(BATCH,),
        in_specs=[pl.BlockSpec((None, H_B, SEQ), lambda b: (b, 0, 0))],
        out_specs=pl.BlockSpec((None, H_B, SEQ), lambda b: (b, 0, 0)),
        out_shape=jax.ShapeDtypeStruct((BATCH, H_B, SEQ), f32),
        compiler_params=_cparams(("parallel",)),
        name="fox_cumsum_prompt",
    )(lft)


FL_T = 512


def _flash_body(*refs, G, DQ, DV, has_bias, nk):
    if has_bias:
        q_ref, k_ref, v_ref, fq_ref, fk_ref, o_ref, m_scr, l_scr, acc_scr = refs
    else:
        q_ref, k_ref, v_ref, o_ref, m_scr, l_scr, acc_scr = refs
    i = pl.program_id(2)
    j = pl.program_id(3)

    @pl.when(j == 0)
    def _():
        m_scr[...] = jnp.full(m_scr.shape, NEG_INF, f32)
        l_scr[...] = jnp.zeros(l_scr.shape, f32)
        acc_scr[...] = jnp.zeros(acc_scr.shape, f32)

    @pl.when(j <= i)
    def _():
        k = k_ref[...].astype(bf16)
        v = v_ref[...].astype(bf16)
        row = i * FL_T + lax.broadcasted_iota(i32, (FL_T, FL_T), 0)
        col = j * FL_T + lax.broadcasted_iota(i32, (FL_T, FL_T), 1)
        mask = col <= row
        for g in range(G):
            s = lax.dot_general(q_ref[:, g * DQ:(g + 1) * DQ], k, NT, preferred_element_type=f32)
            if has_bias:
                s = s + (fq_ref[:, g:g + 1] - fk_ref[g:g + 1, :])
            s = jnp.where(mask, s, NEG_INF)
            m_prev = m_scr[g]
            m_new = jnp.maximum(m_prev, jnp.max(s, axis=-1, keepdims=True))
            alpha = jnp.exp(m_prev - m_new)
            p = jnp.exp(s - m_new)
            l_scr[g] = alpha * l_scr[g] + jnp.sum(p, axis=-1, keepdims=True)
            acc_scr[g] = alpha * acc_scr[g] + jnp.dot(p.astype(bf16), v, preferred_element_type=f32)
            m_scr[g] = m_new

    @pl.when(j == nk - 1)
    def _():
        for g in range(G):
            o_ref[:, g * DV:(g + 1) * DV] = (acc_scr[g] / l_scr[g]).astype(o_ref.dtype)


def _flash_prompt(q, k, v, n_kv, G, DQ, DV, fq=None, fk=None):
    nq = SEQ // FL_T
    has_bias = fq is not None
    qmap = lambda b, h, i, j: (b * nq + i, h)
    kmap = lambda b, h, i, j: (b * nq + jnp.minimum(i, j), h)
    in_specs = [pl.BlockSpec((FL_T, G * DQ), qmap), pl.BlockSpec((FL_T, DQ), kmap), pl.BlockSpec((FL_T, DV), kmap)]
    ins = [q, k, v]
    if has_bias:
        in_specs += [pl.BlockSpec((None, FL_T, G), lambda b, h, i, j: (h, b * nq + i, 0)),
                     pl.BlockSpec((None, G, FL_T), lambda b, h, i, j: (b * n_kv + h, 0, jnp.minimum(i, j)))]
        ins += [fq, fk]
    return pl.pallas_call(
        functools.partial(_flash_body, G=G, DQ=DQ, DV=DV, has_bias=has_bias, nk=nq),
        grid=(BATCH, n_kv, nq, nq),
        in_specs=in_specs,
        out_specs=pl.BlockSpec((FL_T, G * DV), qmap),
        out_shape=jax.ShapeDtypeStruct((MP, n_kv * G * DV), bf16),
        scratch_shapes=[pltpu.VMEM((G, FL_T, 1), f32), pltpu.VMEM((G, FL_T, 1), f32), pltpu.VMEM((G, FL_T, DV), f32)],
        compiler_params=_cparams(("parallel", "parallel", "parallel", "arbitrary")),
        name="flash_prompt_bias" if has_bias else "flash_prompt",
    )(*ins)


def _page_copies(pt_ref, layer, b, c, slot, pages, specs):
    out = []
    for g in range(pages):
        page = pt_ref[b, c * pages + g]
        for hbm_ref, buf_ref, sem_ref, rows in specs:
            out.append(pltpu.make_async_copy(hbm_ref.at[layer, page],
                                             buf_ref.at[slot, pl.ds(g * rows, rows), :],
                                             sem_ref.at[slot]))
    return out


def _gather_pipeline(pt_ref, layer, n_chunks, pages, specs):
    b, c = pl.program_id(0), pl.program_id(1)
    n = b * n_chunks + c
    total = pl.num_programs(0) * n_chunks
    slot = n % 2

    @pl.when(n == 0)
    def _():
        for cp in _page_copies(pt_ref, layer, 0, 0, 0, pages, specs):
            cp.start()

    nxt = n + 1

    @pl.when(nxt < total)
    def _():
        for cp in _page_copies(pt_ref, layer, nxt // n_chunks, nxt % n_chunks, nxt % 2, pages, specs):
            cp.start()

    for cp in _page_copies(pt_ref, layer, b, c, slot, pages, specs):
        cp.wait()
    return slot


def _softmax_update(s, v, m_ref, l_ref, acc_ref):
    m_prev = m_ref[...]
    m_new = jnp.maximum(m_prev, jnp.max(s, axis=-1, keepdims=True))
    alpha = jnp.exp(m_prev - m_new)
    p = jnp.exp(s - m_new)
    l_ref[...] = alpha * l_ref[...] + jnp.sum(p, axis=-1, keepdims=True)
    acc_ref[...] = alpha * acc_ref[...] + jnp.dot(p.astype(bf16), v, preferred_element_type=f32)
    m_ref[...] = m_new


def _new_key_mask(rows):
    t = lax.broadcasted_iota(i32, (rows, LANES), 0) % DEC_SEQ
    col = lax.broadcasted_iota(i32, (rows, LANES), 1)
    return (col < DEC_SEQ) & (col <= t)


LF_ROWS = PAST_LEN + CS_CHUNK


def _lf_sample_body(pt_ref, cache_ref, lfn_ref, fk_ref, fn_ref, buf, sem, *, layer):
    n = pl.program_id(0)

    @pl.when(n == 0)
    def _():
        buf[:, PAST_LEN:LF_ROWS, :] = jnp.zeros((2, CS_CHUNK, H_B), f32)

    slot = _gather_pipeline(pt_ref, layer, 1, N_PAGES, [(cache_ref, buf, sem, PAGE_SIZE)])
    buf[slot, PAST_LEN:PAST_LEN + DEC_SEQ, :] = lfn_ref[...]

    eye = (lax.broadcasted_iota(i32, (H_B, H_B), 0) == lax.broadcasted_iota(i32, (H_B, H_B), 1))
    eye = jnp.where(eye, 1.0, 0.0).astype(bf16)

    def parts_of(ch):
        x = buf[slot, ch * CS_CHUNK:(ch + 1) * CS_CHUNK, :]
        return [lax.dot_general(eye, p, NT, preferred_element_type=f32).astype(bf16) for p in _split3(x)]

    def write(ch, cs):
        if ch < PAST_LEN // CS_CHUNK:
            fk_ref[:, ch * CS_CHUNK:(ch + 1) * CS_CHUNK] = cs
        else:
            fn_ref[...] = cs[:, 0:LANES]

    _cumsum_chunks(parts_of, LF_ROWS // CS_CHUNK, write, jnp.zeros((H_B, 1), f32))


def _lf_sample(page_table, cache_logf, lf_new, layer):
    grid_spec = pltpu.PrefetchScalarGridSpec(
        num_scalar_prefetch=1,
        grid=(DEC_BATCH, 1),
        in_specs=[pl.BlockSpec(memory_space=pl.ANY),
                  pl.BlockSpec((None, DEC_SEQ, H_B), lambda b, c, pt: (b, 0, 0))],
        out_specs=[pl.BlockSpec((None, H_B, PAST_LEN), lambda b, c, pt: (b, 0, 0)),
                   pl.BlockSpec((None, H_B, LANES), lambda b, c, pt: (b, 0, 0))],
        scratch_shapes=[pltpu.VMEM((2, LF_ROWS, H_B), f32), pltpu.SemaphoreType.DMA((2,))],
    )
    return pl.pallas_call(
        functools.partial(_lf_sample_body, layer=layer),
        grid_spec=grid_spec,
        out_shape=[jax.ShapeDtypeStruct((DEC_BATCH, H_B, PAST_LEN), f32),
                   jax.ShapeDtypeStruct((DEC_BATCH, H_B, LANES), f32)],
        compiler_params=_cparams(("arbitrary", "arbitrary")),
        name="fox_logf_sample",
    )(page_table, cache_logf, lf_new)


DEC_PAGES = 8
DEC_KEYS = DEC_PAGES * PAGE_SIZE
DEC_CHUNKS = N_PAGES // DEC_PAGES
DEC_SUB = MXU_DIM
QROWS = H_A * DEC_SEQ


def _mla_scores(a_ref, qr, c_bf, kr_bf):
    r = lax.dot_general(a_ref[...], c_bf, NT, preferred_element_type=f32)
    rope = lax.dot_general(qr, kr_bf, NT, preferred_element_type=f32)
    out = []
    for h in range(H_A):
        kk = r[h * NOPE_D:(h + 1) * NOPE_D, :]
        inv = lax.rsqrt(jnp.sum(kk * kk, axis=0, keepdims=True) / NOPE_D + EPS)
        lo_, hi_ = H_A * NOPE_D + h * DEC_SEQ, H_A * NOPE_D + (h + 1) * DEC_SEQ
        out.append(r[lo_:hi_, :] * inv)
    return jnp.concatenate(out, axis=0) + rope


def _mla_decode_body(pt_ref, ckv_hbm, kr_hbm, q_ref, cn_ref, krn_ref, wukt_ref, wuv_ref, gkn_ref, o_ref,
                     cbuf, krbuf, csem, krsem, a_scr, qr_scr, s_scr, cn_pad, krn_pad, m_scr, l_scr, acc_scr, *, layer):
    b, c = pl.program_id(0), pl.program_id(1)
    nk = H_A * NOPE_D

    @pl.when((b == 0) & (c == 0))
    def _():
        a_scr[0:nk, :] = wukt_ref[...]
        cn_pad[...] = jnp.zeros(cn_pad.shape, f32)
        krn_pad[...] = jnp.zeros(krn_pad.shape, f32)

    slot = _gather_pipeline(pt_ref, layer, DEC_CHUNKS, DEC_PAGES,
                            [(ckv_hbm, cbuf, csem, PAGE_SIZE), (kr_hbm, krbuf, krsem, PAGE_SIZE)])

    @pl.when(c == 0)
    def _():
        q = q_ref[...]
        wq, qr_rows = [], []
        for h in range(H_A):
            qg = (q[:, h * QH:h * QH + NOPE_D] * gkn_ref[...]).astype(bf16)
            wq.append(jnp.dot(qg, wukt_ref[h * NOPE_D:(h + 1) * NOPE_D, :], preferred_element_type=f32))
            qr_rows.append(q[:, h * QH + NOPE_D:h * QH + NOPE_D + ROPE_D])
        a_scr[nk:nk + QROWS, :] = jnp.concatenate(wq, axis=0).astype(bf16)
        qr_scr[...] = jnp.concatenate(qr_rows, axis=0).astype(bf16)
        m_scr[...] = jnp.full(m_scr.shape, NEG_INF, f32)
        l_scr[...] = jnp.zeros(l_scr.shape, f32)
        acc_scr[...] = jnp.zeros(acc_scr.shape, f32)

    qr = qr_scr[...]
    for sub in range(DEC_KEYS // DEC_SUB):
        rows = pl.ds(sub * DEC_SUB, DEC_SUB)
        s_scr[:, sub * DEC_SUB:(sub + 1) * DEC_SUB] = _mla_scores(
            a_scr, qr, cbuf[slot, rows, :].astype(bf16), krbuf[slot, rows, :].astype(bf16))
    _softmax_update(s_scr[...], cbuf[slot].astype(bf16), m_scr, l_scr, acc_scr)

    @pl.when(c == DEC_CHUNKS - 1)
    def _():
        cn_pad[0:DEC_SEQ, :] = cn_ref[...]
        krn_pad[0:DEC_SEQ, :] = krn_ref[...]
        cn = cn_pad[...].astype(bf16)
        s = _mla_scores(a_scr, qr, cn, krn_pad[...].astype(bf16))
        s = jnp.where(_new_key_mask(QROWS), s, NEG_INF)
        _softmax_update(s, cn, m_scr, l_scr, acc_scr)
        o_lat = (acc_scr[...] / l_scr[...]).astype(bf16)
        for h in range(H_A):
            o_ref[:, h * V_D:(h + 1) * V_D] = jnp.dot(
                o_lat[h * DEC_SEQ:(h + 1) * DEC_SEQ, :], wuv_ref[:, h * V_D:(h + 1) * V_D],
                preferred_element_type=f32).astype(o_ref.dtype)


def _mla_decode(page_table, cache_ckv, cache_krope, q_s, c_new, kr_new, wukt, wuv, gkn, layer):
    seq = lambda w: pl.BlockSpec((None, DEC_SEQ, w), lambda b, c, pt: (b, 0, 0))
    full = lambda a: pl.BlockSpec(a.shape, lambda b, c, pt: (0,) * a.ndim)
    grid_spec = pltpu.PrefetchScalarGridSpec(
        num_scalar_prefetch=1,
        grid=(DEC_BATCH, DEC_CHUNKS),
        in_specs=[pl.BlockSpec(memory_space=pl.ANY), pl.BlockSpec(memory_space=pl.ANY),
                  seq(H_A * QH), seq(KV_LORA), seq(ROPE_D), full(wukt), full(wuv), full(gkn)],
        out_specs=seq(H_A * V_D),
        scratch_shapes=[
            pltpu.VMEM((2, DEC_KEYS, KV_LORA), f32), pltpu.VMEM((2, DEC_KEYS, ROPE_D), f32),
            pltpu.SemaphoreType.DMA((2,)), pltpu.SemaphoreType.DMA((2,)),
            pltpu.VMEM((H_A * NOPE_D + QROWS, KV_LORA), bf16), pltpu.VMEM((QROWS, ROPE_D), bf16),
            pltpu.VMEM((QROWS, DEC_KEYS), f32),
            pltpu.VMEM((LANES, KV_LORA), f32), pltpu.VMEM((LANES, ROPE_D), f32),
            pltpu.VMEM((QROWS, 1), f32), pltpu.VMEM((QROWS, 1), f32), pltpu.VMEM((QROWS, KV_LORA), f32),
        ],
    )
    return pl.pallas_call(
        functools.partial(_mla_decode_body, layer=layer),
        grid_spec=grid_spec,
        out_shape=jax.ShapeDtypeStruct((DEC_BATCH, DEC_SEQ, H_A * V_D), f32),
        compiler_params=_cparams(("arbitrary", "arbitrary")),
        name="mla_decode",
    )(page_table, cache_ckv, cache_krope, q_s, c_new, kr_new, wukt, wuv, gkn)


GROWS = G_B * DEC_SEQ


def _fox_decode_body(pt_ref, k_hbm, v_hbm, q_ref, kn_ref, vn_ref, fq_ref, fk_ref, fn_ref, o_ref,
                     kbuf, vbuf, ksem, vsem, kn_pad, vn_pad, q_scr, m_scr, l_scr, acc_scr, *, layer):
    b, c = pl.program_id(0), pl.program_id(1)

    @pl.when((b == 0) & (c == 0))
    def _():
        kn_pad[...] = jnp.zeros(kn_pad.shape, f32)
        vn_pad[...] = jnp.zeros(vn_pad.shape, f32)

    slot = _gather_pipeline(pt_ref, layer, DEC_CHUNKS, DEC_PAGES,
                            [(k_hbm, kbuf, ksem, PAGE_SIZE * KV_B), (v_hbm, vbuf, vsem, PAGE_SIZE * KV_B)])

    @pl.when(c == 0)
    def _():
        q = q_ref[...]
        q_scr[...] = jnp.concatenate([q[:, h * HD_B:(h + 1) * HD_B] for h in range(H_B)], axis=0).astype(bf16)
        m_scr[...] = jnp.full(m_scr.shape, NEG_INF, f32)
        l_scr[...] = jnp.zeros(l_scr.shape, f32)
        acc_scr[...] = jnp.zeros(acc_scr.shape, f32)

    def bias(fk):
        rows = [fq_ref[h * DEC_SEQ:(h + 1) * DEC_SEQ, :] - fk[h:h + 1, :] for h in range(H_B)]
        return jnp.concatenate(rows, axis=0)

    def attend(kv, k_bf, v_bf, bias_kv, mask):
        rows = slice(kv * GROWS, (kv + 1) * GROWS)
        s = lax.dot_general(q_scr[rows, :], k_bf, NT, preferred_element_type=f32) + bias_kv
        if mask is not None:
            s = jnp.where(mask, s, NEG_INF)
        _softmax_update(s, v_bf, m_scr.at[rows], l_scr.at[rows], acc_scr.at[rows])

    bias_past = bias(fk_ref[...])
    for kv in range(KV_B):
        sel = pl.ds(kv, DEC_KEYS, stride=KV_B)
        attend(kv, kbuf[slot, sel, :].astype(bf16), vbuf[slot, sel, :].astype(bf16),
               bias_past[kv * GROWS:(kv + 1) * GROWS], None)

    @pl.when(c == DEC_CHUNKS - 1)
    def _():
        bias_new = bias(fn_ref[...])
        mask = _new_key_mask(GROWS)
        for kv in range(KV_B):
            kn_pad[0:DEC_SEQ, :] = kn_ref[:, kv * HD_B:(kv + 1) * HD_B]
            vn_pad[0:DEC_SEQ, :] = vn_ref[:, kv * HD_B:(kv + 1) * HD_B]
            attend(kv, kn_pad[...].astype(bf16), vn_pad[...].astype(bf16),
                   bias_new[kv * GROWS:(kv + 1) * GROWS], mask)
        o = acc_scr[...] / l_scr[...]
        for h in range(H_B):
            o_ref[:, h * HD_B:(h + 1) * HD_B] = o[h * DEC_SEQ:(h + 1) * DEC_SEQ, :].astype(o_ref.dtype)


def _fox_decode(page_table, cache_k, cache_v, q_s, k_new, v_new, fq_col, fk_past, fn_new, layer):
    seq = lambda w: pl.BlockSpec((None, DEC_SEQ, w), lambda b, c, pt: (b, 0, 0))
    grid_spec = pltpu.PrefetchScalarGridSpec(
        num_scalar_prefetch=1,
        grid=(DEC_BATCH, DEC_CHUNKS),
        in_specs=[pl.BlockSpec(memory_space=pl.ANY), pl.BlockSpec(memory_space=pl.ANY),
                  seq(H_B * HD_B), seq(KV_B * HD_B), seq(KV_B * HD_B),
                  pl.BlockSpec((None, H_B * DEC_SEQ, 1), lambda b, c, pt: (b, 0, 0)),
                  pl.BlockSpec((None, H_B, DEC_KEYS), lambda b, c, pt: (b, 0, c)),
                  pl.BlockSpec((None, H_B, LANES), lambda b, c, pt: (b, 0, 0))],
        out_specs=seq(H_B * HD_B),
        scratch_shapes=[
            pltpu.VMEM((2, DEC_KEYS * KV_B, HD_B), f32), pltpu.VMEM((2, DEC_KEYS * KV_B, HD_B), f32),
            pltpu.SemaphoreType.DMA((2,)), pltpu.SemaphoreType.DMA((2,)),
            pltpu.VMEM((LANES, HD_B), f32), pltpu.VMEM((LANES, HD_B), f32),
            pltpu.VMEM((H_B * DEC_SEQ, HD_B), bf16),
            pltpu.VMEM((H_B * DEC_SEQ, 1), f32), pltpu.VMEM((H_B * DEC_SEQ, 1), f32),
            pltpu.VMEM((H_B * DEC_SEQ, HD_B), f32),
        ],
    )
    return pl.pallas_call(
        functools.partial(_fox_decode_body, layer=layer),
        grid_spec=grid_spec,
        out_shape=jax.ShapeDtypeStruct((DEC_BATCH, DEC_SEQ, H_B * HD_B), f32),
        compiler_params=_cparams(("arbitrary", "arbitrary")),
        name="fox_decode",
    )(page_table, cache_k, cache_v, q_s, k_new, v_new, fq_col, fk_past, fn_new)


C_TM = 256
C_IN = (H_C + 2 * KV_C) * HD_C


def _norm_pair_tile(t, g128, lo):
    ss = t * t
    s_lo = jnp.sum(jnp.where(lo, ss, 0.0), axis=-1, keepdims=True)
    s_hi = jnp.sum(jnp.where(lo, 0.0, ss), axis=-1, keepdims=True)
    r = jnp.where(lo, lax.rsqrt(s_lo / 64.0 + EPS), lax.rsqrt(s_hi / 64.0 + EPS))
    return (t * r) * g128


def _cproj_body(x_ref, mg_ref, win_ref, gq_ref, gk_ref, q_ref, k_ref, v_ref):
    n = _rms(x_ref[...], mg_ref[...]).astype(bf16)
    y = jnp.dot(n, win_ref[...], preferred_element_type=f32)
    lo, _ = _half_masks((C_TM, LANES))
    nq = H_C * HD_C
    for p in range(nq // LANES):
        q_ref[:, p * LANES:(p + 1) * LANES] = (
            _norm_pair_tile(y[:, p * LANES:(p + 1) * LANES], gq_ref[...], lo) * C_SCALE).astype(bf16)
    for p in range(KV_C * HD_C // LANES):
        k_ref[:, p * LANES:(p + 1) * LANES] = _norm_pair_tile(y[:, nq + p * LANES:nq + (p + 1) * LANES], gk_ref[...], lo)
    v_ref[...] = y[:, nq + KV_C * HD_C:]


def _cproj(x, mg, win, gq128, gk128):
    m = x.shape[0]
    row = lambda w: pl.BlockSpec((C_TM, w), lambda i: (i, 0))
    full = lambda a: pl.BlockSpec(a.shape, lambda i: (0,) * a.ndim)
    return pl.pallas_call(
        _cproj_body,
        grid=(m // C_TM,),
        in_specs=[row(D_MODEL), full(mg), full(win), full(gq128), full(gk128)],
        out_specs=[row(H_C * HD_C), row(KV_C * HD_C), row(KV_C * HD_C)],
        out_shape=[jax.ShapeDtypeStruct((m, H_C * HD_C), bf16), jax.ShapeDtypeStruct((m, KV_C * HD_C), f32),
                   jax.ShapeDtypeStruct((m, KV_C * HD_C), f32)],
        compiler_params=_cparams(("parallel",)),
        name="swa_proj",
    )(x, mg, win, gq128, gk128)


def _t5_bucket_np(dist):
    n = np.maximum(dist, 0)
    nf = np.maximum(n, 1).astype(np.float32)
    large = MAX_EXACT + (np.log(nf / np.float32(MAX_EXACT)) / np.float32(math.log(MAX_DISTANCE / MAX_EXACT))
                         * np.float32(NUM_BUCKETS - MAX_EXACT)).astype(np.int32)
    return np.where(n < MAX_EXACT, n, np.minimum(large, NUM_BUCKETS - 1)).astype(np.int32)


def _fill_bias(rel_ref, bucket, bias_scr):
    def one_head(h, carry):
        acc = jnp.zeros(bucket.shape, f32)
        for bk in range(NUM_BUCKETS):
            acc = jnp.where(bucket == bk, rel_ref[bk, h], acc)
        bias_scr[h] = acc
        return carry
    lax.fori_loop(0, H_C, one_head, 0)


def _sink_softmax_pv(s, sink, v_bf):
    m = jnp.maximum(jnp.max(s, axis=-1, keepdims=True), sink)
    e = jnp.exp(s - m)
    den = jnp.sum(e, axis=-1, keepdims=True) + jnp.exp(sink - m)
    return jnp.dot((e / den).astype(bf16), v_bf, preferred_element_type=f32)


def _swa_prompt_body(rel_ref, sink_ref, bucket_ref, q_ref, kp_ref, kc_ref, vp_ref, vc_ref, o_ref, bias_scr):
    b, n = pl.program_id(0), pl.program_id(1)

    @pl.when((b == 0) & (n == 0))
    def _():
        _fill_bias(rel_ref, bucket_ref[...], bias_scr)

    kb = jnp.concatenate([kp_ref[...], kc_ref[...]], axis=0).astype(bf16)
    vb = jnp.concatenate([vp_ref[...], vc_ref[...]], axis=0).astype(bf16)
    row = lax.broadcasted_iota(i32, (WINDOW, 2 * WINDOW), 0) + WINDOW
    col = lax.broadcasted_iota(i32, (WINDOW, 2 * WINDOW), 1)
    dist = row - col
    mask = (dist >= 0) & (dist < WINDOW) & ((n > 0) | (col >= WINDOW))
    for kv in range(KV_C):
        k_h = kb[:, kv * HD_C:(kv + 1) * HD_C]
        v_h = vb[:, kv * HD_C:(kv + 1) * HD_C]
        outs = []
        for g in range(G_C):
            h = kv * G_C + g
            s = lax.dot_general(q_ref[:, h * HD_C:(h + 1) * HD_C], k_h, NT, preferred_element_type=f32) + bias_scr[h]
            s = jnp.where(mask, s, NEG_INF)
            outs.append(_sink_softmax_pv(s, sink_ref[h], v_h))
        for p in range(G_C // 2):
            tile = jnp.concatenate([outs[2 * p], outs[2 * p + 1]], axis=1)
            col0 = (kv * G_C + 2 * p) * HD_C
            o_ref[:, col0:col0 + LANES] = tile.astype(o_ref.dtype)


def _swa_prompt(q, k, v, rel_bias, sinks):
    nb = SEQ // WINDOW
    bucket = jnp.asarray(_t5_bucket_np(WINDOW + np.arange(WINDOW)[:, None] - np.arange(2 * WINDOW)[None, :]))
    smem = pl.BlockSpec(memory_space=pltpu.SMEM)
    cur = lambda w: pl.BlockSpec((WINDOW, w), lambda b, n: (b * nb + n, 0))
    prev = lambda w: pl.BlockSpec((WINDOW, w), lambda b, n: (b * nb + jnp.maximum(n - 1, 0), 0))
    kvw = KV_C * HD_C
    return pl.pallas_call(
        _swa_prompt_body,
        grid=(BATCH, nb),
        in_specs=[smem, smem, pl.BlockSpec(bucket.shape, lambda b, n: (0, 0)),
                  cur(H_C * HD_C), prev(kvw), cur(kvw), prev(kvw), cur(kvw)],
        out_specs=cur(H_C * HD_C),
        out_shape=jax.ShapeDtypeStruct((MP, H_C * HD_C), bf16),
        scratch_shapes=[pltpu.VMEM((H_C, WINDOW, 2 * WINDOW), f32)],
        compiler_params=_cparams(("arbitrary", "arbitrary")),
        name="swa_prompt",
    )(rel_bias, sinks, bucket, q, k, k, v, v)


CROWS = G_C * DEC_SEQ


def _swa_decode_body(rel_ref, sink_ref, bucket_ref, q_ref, kb_ref, vb_ref, kn_ref, vn_ref,
                     o_ref, ko_ref, vo_ref, bias_scr, kn_pad, vn_pad):
    b = pl.program_id(0)

    @pl.when(b == 0)
    def _():
        _fill_bias(rel_ref, bucket_ref[...], bias_scr)
        kn_pad[...] = jnp.zeros(kn_pad.shape, f32)
        vn_pad[...] = jnp.zeros(vn_pad.shape, f32)

    kn_pad[0:DEC_SEQ, :] = kn_ref[...]
    vn_pad[0:DEC_SEQ, :] = vn_ref[...]
    keys = jnp.concatenate([kb_ref[...], kn_pad[...]], axis=0).astype(bf16)
    vals = jnp.concatenate([vb_ref[...], vn_pad[...]], axis=0).astype(bf16)
    q = q_ref[...]

    t = lax.broadcasted_iota(i32, (DEC_SEQ, 2 * WINDOW), 0)
    col = lax.broadcasted_iota(i32, (DEC_SEQ, 2 * WINDOW), 1)
    dist = WINDOW + t - col
    mask = (dist >= 0) & (dist < WINDOW)
    for kv in range(KV_C):
        k_h = keys[:, kv * HD_C:(kv + 1) * HD_C]
        v_h = vals[:, kv * HD_C:(kv + 1) * HD_C]
        q_kv = jnp.concatenate([q[:, (kv * G_C + g) * HD_C:(kv * G_C + g + 1) * HD_C] for g in range(G_C)],
                               axis=0).astype(bf16)
        s_all = lax.dot_general(q_kv, k_h, NT, preferred_element_type=f32)
        outs = []
        for g in range(G_C):
            h = kv * G_C + g
            s = s_all[g * DEC_SEQ:(g + 1) * DEC_SEQ, :] + bias_scr[h]
            s = jnp.where(mask, s, NEG_INF)
            outs.append(_sink_softmax_pv(s, sink_ref[h], v_h))
        for p in range(G_C // 2):
            tile = jnp.concatenate([outs[2 * p], outs[2 * p + 1]], axis=1)
            col0 = (kv * G_C + 2 * p) * HD_C
            o_ref[:, col0:col0 + LANES] = tile.astype(o_ref.dtype)

    keep = WINDOW - DEC_SEQ
    ko_ref[0:keep, :] = kb_ref[DEC_SEQ:WINDOW, :]
    ko_ref[keep:WINDOW, :] = kn_ref[...]
    vo_ref[0:keep, :] = vb_ref[DEC_SEQ:WINDOW, :]
    vo_ref[keep:WINDOW, :] = vn_ref[...]


def _swa_decode(q_s, buf_k, buf_v, k_new, v_new, rel_bias, sinks):
    bucket = jnp.asarray(_t5_bucket_np(WINDOW + np.arange(DEC_SEQ)[:, None] - np.arange(2 * WINDOW)[None, :]))
    smem = pl.BlockSpec(memory_space=pltpu.SMEM)
    kvw = KV_C * HD_C
    seq = lambda r, w: pl.BlockSpec((None, r, w), lambda b: (b, 0, 0))
    return pl.pallas_call(
        _swa_decode_body,
        grid=(DEC_BATCH,),
        in_specs=[smem, smem, pl.BlockSpec(bucket.shape, lambda b: (0, 0)),
                  seq(DEC_SEQ, H_C * HD_C), seq(WINDOW, kvw), seq(WINDOW, kvw), seq(DEC_SEQ, kvw), seq(DEC_SEQ, kvw)],
        out_specs=[seq(DEC_SEQ, H_C * HD_C), seq(WINDOW, kvw), seq(WINDOW, kvw)],
        out_shape=[jax.ShapeDtypeStruct((DEC_BATCH, DEC_SEQ, H_C * HD_C), f32),
                   jax.ShapeDtypeStruct((DEC_BATCH, WINDOW, kvw), f32),
                   jax.ShapeDtypeStruct((DEC_BATCH, WINDOW, kvw), f32)],
        scratch_shapes=[pltpu.VMEM((H_C, DEC_SEQ, 2 * WINDOW), f32), pltpu.VMEM((WINDOW, kvw), f32),
                        pltpu.VMEM((WINDOW, kvw), f32)],
        compiler_params=_cparams(("arbitrary",)),
        name="swa_decode",
    )(rel_bias, sinks, bucket, q_s, buf_k, buf_v, k_new, v_new)


def _rope_tables():
    half = ROPE_D // 2
    inv = ROPE_THETA ** (-jnp.arange(half, dtype=f32) / half)
    pos = jnp.concatenate([jnp.tile(jnp.arange(SEQ, dtype=i32), BATCH),
                           jnp.tile(PAST_LEN + jnp.arange(DEC_SEQ, dtype=i32), DEC_BATCH)])
    ang = pos.astype(f32)[:, None] * inv[None, :]
    cos, sin = jnp.cos(ang), jnp.sin(ang)
    return jnp.tile(cos, (1, 4)), jnp.tile(jnp.concatenate([-sin, sin], axis=1), (1, 2))


def _row(v, reps=1):
    return jnp.tile(v.astype(f32), reps).reshape(1, -1)


def _ab_layer(x, j, tabs, caches, page_table, p):
    cos128, sin128 = tabs
    cache_ckv, cache_krope, cache_k, cache_v, cache_logf = caches
    w_in = p["ab_w_in"][j]
    pad = jnp.zeros((D_MODEL, IN_AB_PAD - w_in.shape[1]), f32)
    win = jnp.concatenate([w_in[:, 0:1024], w_in[:, 1088:2624], w_in[:, 1024:1088], w_in[:, 2624:2632], pad],
                          axis=1).astype(bf16)
    wq = p["mla_w_q_up"][j].reshape(Q_LORA, H_A, NOPE_D + ROPE_D)
    wqu = jnp.concatenate([wq[:, :, :NOPE_D].reshape(Q_LORA, -1), wq[:, :, NOPE_D:].reshape(Q_LORA, -1)],
                          axis=1).astype(bf16)
    bf128 = jnp.zeros((1, LANES), f32).at[0, 64:64 + H_B].set(p["fox_b_f"][j])
    gkr128 = jnp.concatenate([p["mla_g_kr"][j], jnp.ones((64,), f32)]).reshape(1, LANES)
    qmla, ckv, krope, kr128, fq, fk, fv, logf = _abproj(
        x, _row(p["mix_g"][2 * j]), win, _row(p["mla_g_q_lat"][j]), wqu, _row(p["mla_g_kv_lat"][j]),
        _row(p["mla_g_qn"][j]), _row(p["mla_g_qr"][j], 2), gkr128, _row(p["fox_g_q"][j]), _row(p["fox_g_k"][j]),
        bf128, cos128, sin128)
    wuk = p["mla_w_uk"][j].astype(bf16)
    wuv = p["mla_w_uv"][j].astype(bf16)
    gkn = _row(p["mla_g_kn"][j])

    kmla, va = _kvup(ckv, kr128, wuk, wuv, gkn)
    o_a = _flash_prompt(qmla, kmla, va, H_A, 1, QH, V_D)
    lf_p = logf[:MP].reshape(BATCH, SEQ, H_B)
    f_t = _cumsum_prompt(jnp.swapaxes(lf_p, 1, 2))
    fq_tok = jnp.swapaxes(f_t, 1, 2).reshape(MP, KV_B, G_B).swapaxes(0, 1)
    o_b = _flash_prompt(fq, fk, fv, KV_B, G_B, HD_B, HD_B, fq=fq_tok,
                        fk=f_t.reshape(BATCH * KV_B, G_B, SEQ))
    o_p = jnp.concatenate([o_a, o_b], axis=1)

    seqs = lambda a: a[MP:].reshape(DEC_BATCH, DEC_SEQ, -1)
    fk_past, fn_new = _lf_sample(page_table, cache_logf, seqs(logf), j)
    fq_col = fn_new[:, :, :DEC_SEQ].reshape(DEC_BATCH, H_B * DEC_SEQ, 1)
    os_a = _mla_decode(page_table, cache_ckv, cache_krope, seqs(qmla).astype(f32), seqs(ckv), seqs(krope),
                       wuk.T, wuv, gkn, j)
    os_b = _fox_decode(page_table, cache_k, cache_v, seqs(fq).astype(f32), seqs(fk), seqs(fv),
                       fq_col, fk_past, fn_new, j)
    o_s = jnp.concatenate([os_a, os_b], axis=2).reshape(MS, D_MODEL).astype(bf16)

    x = _outproj(x, jnp.concatenate([o_p, o_s], axis=0), p["ab_w_out"][j].astype(bf16))
    st_p = (ckv[:MP].reshape(BATCH, SEQ, KV_LORA), krope[:MP].reshape(BATCH, SEQ, ROPE_D),
            fk[:MP].reshape(BATCH, SEQ, KV_B, HD_B), fv[:MP].reshape(BATCH, SEQ, KV_B, HD_B), lf_p)
    st_s = (seqs(ckv), seqs(krope), seqs(fk).reshape(DEC_BATCH, DEC_SEQ, KV_B, HD_B),
            seqs(fv).reshape(DEC_BATCH, DEC_SEQ, KV_B, HD_B), seqs(logf))
    return x, st_p, st_s


def _c_layer(x, j, state_k, state_v, p):
    kvw = KV_C * HD_C
    q, k, v = _cproj(x, _row(p["mix_g"][2 * j + 1]), p["swa_w_in"][j].astype(bf16),
                     _row(p["swa_g_q"][j], 2), _row(p["swa_g_k"][j], 2))
    rel = p["rel_bias"].astype(f32)
    sinks = p["swa_sinks"][j].astype(f32)
    o_p = _swa_prompt(q, k, v, rel, sinks)
    seqs = lambda a: a[MP:].reshape(DEC_BATCH, DEC_SEQ, -1)
    o_s, k_out, v_out = _swa_decode(seqs(q).astype(f32), state_k[j].reshape(DEC_BATCH, WINDOW, kvw),
                                    state_v[j].reshape(DEC_BATCH, WINDOW, kvw), seqs(k), seqs(v), rel, sinks)
    o = jnp.concatenate([o_p, o_s.reshape(MS, D_MODEL).astype(bf16)], axis=0)
    x = _outproj(x, o, p["swa_w_out"][j].astype(bf16))
    tail = lambda a: a[:MP].reshape(BATCH, SEQ, KV_C, HD_C)[:, SEQ - WINDOW:]
    st_p = (tail(k), tail(v))
    st_s = (k_out.reshape(DEC_BATCH, WINDOW, KV_C, HD_C), v_out.reshape(DEC_BATCH, WINDOW, KV_C, HD_C))
    return x, st_p, st_s


def kernel(x_prompt, x_sample, cache_mla_ckv, cache_mla_krope, cache_fox_k, cache_fox_v, cache_fox_logf, state_swa_k, state_swa_v, page_table, ffn1_g, ffn1_w_gate, ffn1_w_up, ffn1_w_down, mix_g, ffn2_g, ffn2_w_gate, ffn2_w_up, ffn2_w_down, ab_w_in, mla_g_q_lat, mla_w_q_up, mla_g_kv_lat, mla_w_uk, mla_w_uv, mla_g_qn, mla_g_qr, mla_g_kn, mla_g_kr, fox_g_q, fox_g_k, fox_b_f, ab_w_out, swa_w_in, swa_g_q, swa_g_k, swa_sinks, swa_w_out, rel_bias):
    assert WINDOW == PAGE_SIZE == LANES and min(WINDOW, PAST_LEN) == WINDOW
    p = dict(mix_g=mix_g, ab_w_in=ab_w_in, mla_g_q_lat=mla_g_q_lat, mla_w_q_up=mla_w_q_up,
             mla_g_kv_lat=mla_g_kv_lat, mla_w_uk=mla_w_uk, mla_w_uv=mla_w_uv, mla_g_qn=mla_g_qn,
             mla_g_qr=mla_g_qr, mla_g_kn=mla_g_kn, mla_g_kr=mla_g_kr, fox_g_q=fox_g_q, fox_g_k=fox_g_k,
             fox_b_f=fox_b_f, ab_w_out=ab_w_out, swa_w_in=swa_w_in, swa_g_q=swa_g_q, swa_g_k=swa_g_k,
             swa_sinks=swa_sinks, swa_w_out=swa_w_out, rel_bias=rel_bias)
    n_pool = cache_fox_k.shape[1]
    caches = (cache_mla_ckv, cache_mla_krope,
              cache_fox_k.reshape(-1, n_pool, PAGE_SIZE * KV_B, HD_B),
              cache_fox_v.reshape(-1, n_pool, PAGE_SIZE * KV_B, HD_B), cache_fox_logf)
    tabs = _rope_tables()
    x = jnp.concatenate([x_prompt.reshape(MP, D_MODEL), x_sample.reshape(MS, D_MODEL)], axis=0)
    ab_p, ab_s, c_p, c_s = [], [], [], []
    for l in range(DEPTH):
        x = _ffn(x, ffn1_g[l], ffn1_w_gate[l].astype(bf16), ffn1_w_up[l].astype(bf16), ffn1_w_down[l].astype(bf16))
        j = l // 2
        if l % 2 == 0:
            x, st_p, st_s = _ab_layer(x, j, tabs, caches, page_table, p)
            ab_p.append(st_p)
            ab_s.append(st_s)
        else:
            x, st_p, st_s = _c_layer(x, j, state_swa_k, state_swa_v, p)
            c_p.append(st_p)
            c_s.append(st_s)
        x = _ffn(x, ffn2_g[l], ffn2_w_gate[l].astype(bf16), ffn2_w_up[l].astype(bf16), ffn2_w_down[l].astype(bf16))

    def stack(states, i):
        return jnp.stack([st[i] for st in states], axis=0)

    return (x[:MP].reshape(BATCH, SEQ, D_MODEL), x[MP:].reshape(DEC_BATCH, DEC_SEQ, D_MODEL),
            stack(ab_p, 0), stack(ab_p, 1), stack(ab_p, 2), stack(ab_p, 3), stack(ab_p, 4),
            stack(c_p, 0), stack(c_p, 1),
            stack(ab_s, 0), stack(ab_s, 1), stack(ab_s, 2), stack(ab_s, 3), stack(ab_s, 4),
            stack(c_s, 0), stack(c_s, 1))
```

```python
import functools
import math

import numpy as np
import jax
import jax.numpy as jnp
from jax import lax
from jax.experimental import pallas as pl
from jax.experimental.pallas import tpu as pltpu

f32, bf16, i32 = jnp.float32, jnp.bfloat16, jnp.int32

D_MODEL = 2048
BATCH = 4
SEQ = 2048
DEPTH = 4
DEC_BATCH = 128
DEC_SEQ = 8
PAST_LEN = 8192
PAGE_SIZE = 128
N_PAGES = PAST_LEN // PAGE_SIZE
H_A = 8
Q_LORA = 512
KV_LORA = 512
NOPE_D = 128
ROPE_D = 64
V_D = 128
ROPE_THETA = 10000.0
H_B = 8
KV_B = 2
HD_B = 128
H_C = 32
KV_C = 4
HD_C = 64
WINDOW = 128
NUM_BUCKETS = 32
MAX_DISTANCE = 128
D_FF = 4096
FFN_RES = 0.5
EPS = 1e-6
NEG_INF = -1e30
MLA_SCALE = (NOPE_D + ROPE_D) ** -0.5
FOX_SCALE = HD_B ** -0.5
C_SCALE = HD_C ** -0.5
G_B = H_B // KV_B
G_C = H_C // KV_C
MAX_EXACT = NUM_BUCKETS // 2

MP = BATCH * SEQ
MS = DEC_BATCH * DEC_SEQ
MT = MP + MS
IN_AB_PAD = 2688
QH = 256

LANES = 128
MXU_DIM = 256
VMEM_LIMIT = 56 * 1024 * 1024

NT = (((1,), (1,)), ((), ()))


def _cparams(sem):
    return pltpu.CompilerParams(dimension_semantics=sem, vmem_limit_bytes=VMEM_LIMIT)


def _rms(x, g, eps=EPS):
    return (x * lax.rsqrt(jnp.mean(x * x, axis=-1, keepdims=True) + eps)) * g


def _split3(x):
    hi = x.astype(bf16)
    r1 = x - hi.astype(f32)
    mid = r1.astype(bf16)
    lo = (r1 - mid.astype(f32)).astype(bf16)
    return hi, mid, lo


FFN_TM = 512
FFN_TF = 512


def _ffn_body(x_ref, g_ref, wg_ref, wu_ref, wd_ref, o_ref, n_ref):
    @pl.when(pl.program_id(1) == 0)
    def _():
        x = x_ref[...]
        n_ref[...] = _rms(x, g_ref[...]).astype(bf16)
        o_ref[...] = x

    n = n_ref[...]
    gate = jnp.dot(n, wg_ref[...], preferred_element_type=f32)
    up = jnp.dot(n, wu_ref[...], preferred_element_type=f32)
    h = (gate / (1.0 + jnp.exp(-gate))) * up * FFN_RES
    o_ref[...] += jnp.dot(h.astype(bf16), wd_ref[...], preferred_element_type=f32)


def _ffn(x, g, wg, wu, wd):
    m = x.shape[0]
    return pl.pallas_call(
        _ffn_body,
        grid=(m // FFN_TM, D_FF // FFN_TF),
        in_specs=[
            pl.BlockSpec((FFN_TM, D_MODEL), lambda i, f: (i, 0)),
            pl.BlockSpec((1, D_MODEL), lambda i, f: (0, 0)),
            pl.BlockSpec((D_MODEL, FFN_TF), lambda i, f: (0, f)),
            pl.BlockSpec((D_MODEL, FFN_TF), lambda i, f: (0, f)),
            pl.BlockSpec((FFN_TF, D_MODEL), lambda i, f: (f, 0)),
        ],
        out_specs=pl.BlockSpec((FFN_TM, D_MODEL), lambda i, f: (i, 0)),
        out_shape=jax.ShapeDtypeStruct((m, D_MODEL), f32),
        scratch_shapes=[pltpu.VMEM((FFN_TM, D_MODEL), bf16)],
        compiler_params=_cparams(("parallel", "arbitrary")),
        name="ffn_half",
    )(x, g.reshape(1, D_MODEL), wg, wu, wd)


OUT_TM = 512


def _outproj_body(x_ref, o_ref, w_ref, y_ref):
    y_ref[...] = x_ref[...] + jnp.dot(o_ref[...], w_ref[...], preferred_element_type=f32)


def _outproj(x, o, w):
    m = x.shape[0]
    return pl.pallas_call(
        _outproj_body,
        grid=(m // OUT_TM,),
        in_specs=[
            pl.BlockSpec((OUT_TM, D_MODEL), lambda i: (i, 0)),
            pl.BlockSpec((OUT_TM, D_MODEL), lambda i: (i, 0)),
            pl.BlockSpec((D_MODEL, D_MODEL), lambda i: (0, 0)),
        ],
        out_specs=pl.BlockSpec((OUT_TM, D_MODEL), lambda i: (i, 0)),
        out_shape=jax.ShapeDtypeStruct((m, D_MODEL), f32),
        compiler_params=_cparams(("parallel",)),
        name="out_proj",
    )(x, o, w)


AB_TM = 256


def _half_masks(shape):
    lane = lax.broadcasted_iota(i32, shape, 1)
    return lane < 64, (lane % 64) < 32


def _norm_rope_tile(t, g128, cosv, sinv, lo, first):
    ss = t * t
    s_lo = jnp.sum(jnp.where(lo, ss, 0.0), axis=-1, keepdims=True)
    s_hi = jnp.sum(jnp.where(lo, 0.0, ss), axis=-1, keepdims=True)
    r = jnp.where(lo, lax.rsqrt(s_lo / 64.0 + EPS), lax.rsqrt(s_hi / 64.0 + EPS))
    y = (t * r) * g128
    other = jnp.where(first, pltpu.roll(y, 96, 1), pltpu.roll(y, 32, 1))
    return y * cosv + other * sinv


def _abproj_body(x_ref, mg_ref, win_ref, gql_ref, wqu_ref, gkv_ref, gqn_ref, gqr_ref, gkr_ref,
                 gfq_ref, gfk_ref, bf_ref, cos_ref, sin_ref,
                 qmla_ref, ckv_ref, krope_ref, kr128_ref, fq_ref, fk_ref, fv_ref, logf_ref):
    n = _rms(x_ref[...], mg_ref[...]).astype(bf16)
    y = jnp.dot(n, win_ref[...], preferred_element_type=f32)
    cosv, sinv = cos_ref[...], sin_ref[...]
    lo, first = _half_masks(cosv.shape)

    ql = _rms(y[:, 0:Q_LORA], gql_ref[...]).astype(bf16)
    q = jnp.dot(ql, wqu_ref[...], preferred_element_type=f32)
    for h in range(H_A):
        t = _rms(q[:, h * NOPE_D:(h + 1) * NOPE_D], gqn_ref[...]) * MLA_SCALE
        qmla_ref[:, h * QH:h * QH + NOPE_D] = t.astype(bf16)
    base = H_A * NOPE_D
    for p in range(H_A // 2):
        t = _norm_rope_tile(q[:, base + p * LANES:base + (p + 1) * LANES], gqr_ref[...],
                            cosv, sinv, lo, first) * MLA_SCALE
        even = jnp.where(lo, t, 0.0)
        odd = jnp.where(lo, pltpu.roll(t, 64, 1), 0.0)
        qmla_ref[:, (2 * p) * QH + NOPE_D:(2 * p + 1) * QH] = even.astype(bf16)
        qmla_ref[:, (2 * p + 1) * QH + NOPE_D:(2 * p + 2) * QH] = odd.astype(bf16)

    ckv_ref[...] = _rms(y[:, 512:1024], gkv_ref[...])

    for h in range(H_B):
        t = _rms(y[:, 1024 + h * HD_B:1024 + (h + 1) * HD_B], gfq_ref[...]) * FOX_SCALE
        fq_ref[:, h * HD_B:(h + 1) * HD_B] = t.astype(bf16)
    for k in range(KV_B):
        fk_ref[:, k * HD_B:(k + 1) * HD_B] = _rms(y[:, 2048 + k * HD_B:2048 + (k + 1) * HD_B], gfk_ref[...])
    fv_ref[...] = y[:, 2304:2560]

    t = y[:, 2560:2688]
    kr = _norm_rope_tile(t, gkr_ref[...], cosv, sinv, lo, first)
    krope_ref[...] = kr[:, 0:ROPE_D]
    kr128_ref[...] = jnp.where(lo, kr, 0.0).astype(bf16)
    z = t + bf_ref[...]
    ls = jnp.minimum(z, 0.0) - jnp.log(1.0 + jnp.exp(-jnp.abs(z)))
    logf_ref[...] = ls[:, 64:64 + H_B]


def _abproj(x, mg, win, gql, wqu, gkv, gqn, gqr128, gkr128, gfq, gfk, bf128, cos128, sin128):
    m = x.shape[0]
    row = lambda w: pl.BlockSpec((AB_TM, w), lambda i: (i, 0))
    full = lambda a: pl.BlockSpec(a.shape, lambda i: (0,) * a.ndim)
    ins = [x, mg, win, gql, wqu, gkv, gqn, gqr128, gkr128, gfq, gfk, bf128, cos128, sin128]
    in_specs = [row(D_MODEL)] + [full(a) for a in ins[1:12]] + [row(LANES), row(LANES)]
    outs = [(H_A * QH, bf16), (KV_LORA, f32), (ROPE_D, f32), (LANES, bf16), (H_B * HD_B, bf16),
            (KV_B * HD_B, f32), (KV_B * HD_B, f32), (H_B, f32)]
    return pl.pallas_call(
        _abproj_body,
        grid=(m // AB_TM,),
        in_specs=in_specs,
        out_specs=[row(w) for w, _ in outs],
        out_shape=[jax.ShapeDtypeStruct((m, w), dt) for w, dt in outs],
        compiler_params=_cparams(("parallel",)),
        name="ab_proj",
    )(*ins)


KV_TM = 512


def _kvup_body(ckv_ref, kr128_ref, wuk_ref, wuv_ref, gkn_ref, kmla_ref, va_ref):
    c = ckv_ref[...].astype(bf16)
    kk = jnp.dot(c, wuk_ref[...], preferred_element_type=f32)
    kr = kr128_ref[...]
    for h in range(H_A):
        kmla_ref[:, h * QH:h * QH + NOPE_D] = _rms(kk[:, h * NOPE_D:(h + 1) * NOPE_D], gkn_ref[...]).astype(bf16)
        kmla_ref[:, h * QH + NOPE_D:(h + 1) * QH] = kr
    va_ref[...] = jnp.dot(c, wuv_ref[...], preferred_element_type=f32).astype(bf16)


def _kvup(ckv, kr128, wuk, wuv, gkn):
    m = MP
    row = lambda w: pl.BlockSpec((KV_TM, w), lambda i: (i, 0))
    full = lambda a: pl.BlockSpec(a.shape, lambda i: (0,) * a.ndim)
    return pl.pallas_call(
        _kvup_body,
        grid=(m // KV_TM,),
        in_specs=[row(KV_LORA), row(LANES), full(wuk), full(wuv), full(gkn)],
        out_specs=[row(H_A * QH), row(H_A * V_D)],
        out_shape=[jax.ShapeDtypeStruct((m, H_A * QH), bf16), jax.ShapeDtypeStruct((m, H_A * V_D), bf16)],
        compiler_params=_cparams(("parallel",)),
        name="mla_kv_up",
    )(ckv, kr128, wuk, wuv, gkn)


CS_CHUNK = MXU_DIM


def _tri_ones(n):
    r = lax.broadcasted_iota(i32, (n, n), 0)
    c = lax.broadcasted_iota(i32, (n, n), 1)
    return jnp.where(r <= c, 1.0, 0.0).astype(bf16)


def _cumsum_chunks(parts_of, n_chunks, write, carry):
    tri = _tri_ones(CS_CHUNK)
    for ch in range(n_chunks):
        cs = carry
        for p in parts_of(ch):
            cs = cs + jnp.dot(p, tri, preferred_element_type=f32)
        write(ch, cs)
        carry = cs[:, CS_CHUNK - 1:CS_CHUNK]
    return carry


def _cumsum_body(x_ref, o_ref):
    def parts_of(ch):
        return _split3(x_ref[:, ch * CS_CHUNK:(ch + 1) * CS_CHUNK])

    def write(ch, cs):
        o_ref[:, ch * CS_CHUNK:(ch + 1) * CS_CHUNK] = cs

    _cumsum_chunks(parts_of, SEQ // CS_CHUNK, write, jnp.zeros((H_B, 1), f32))


def _cumsum_prompt(lft):
    return pl.pallas_call(
        _cumsum_body,
        grid=(BATCH,),
        in_specs=[pl.BlockSpec((None, H_B, SEQ), lambda b: (b, 0, 0))],
        out_specs=pl.BlockSpec((None, H_B, SEQ), lambda b: (b, 0, 0)),
        out_shape=jax.ShapeDtypeStruct((BATCH, H_B, SEQ), f32),
        compiler_params=_cparams(("parallel",)),
        name="fox_cumsum_prompt",
    )(lft)


FL_T = 512


def _flash_body(*refs, NH, G, DQ, DV, has_bias, nk):
    if has_bias:
        q_ref, k_ref, v_ref, fq_ref, fk_ref, o_ref, m_scr, l_scr, acc_scr = refs
    else:
        q_ref, k_ref, v_ref, o_ref, m_scr, l_scr, acc_scr = refs
    i = pl.program_id(2)
    j = pl.program_id(3)

    @pl.when(j == 0)
    def _():
        m_scr[...] = jnp.full(m_scr.shape, NEG_INF, f32)
        l_scr[...] = jnp.zeros(l_scr.shape, f32)
        acc_scr[...] = jnp.zeros(acc_scr.shape, f32)

    def tile(diagonal):
        if diagonal:
            mask = (lax.broadcasted_iota(i32, (FL_T, FL_T), 1) <= lax.broadcasted_iota(i32, (FL_T, FL_T), 0))
        for kv in range(NH):
            k = k_ref[:, kv * DQ:(kv + 1) * DQ].astype(bf16)
            v = v_ref[:, kv * DV:(kv + 1) * DV].astype(bf16)
            for g in range(G):
                h = kv * G + g
                s = lax.dot_general(q_ref[:, h * DQ:(h + 1) * DQ], k, NT, preferred_element_type=f32)
                if has_bias:
                    s = s + (fq_ref[:, h:h + 1] - fk_ref[h:h + 1, :])
                if diagonal:
                    s = jnp.where(mask, s, NEG_INF)
                m_prev = m_scr[h]
                m_new = jnp.maximum(m_prev, jnp.max(s, axis=-1, keepdims=True))
                alpha = jnp.exp(m_prev - m_new)
                p = jnp.exp(s - m_new)
                l_scr[h] = alpha * l_scr[h] + jnp.sum(p, axis=-1, keepdims=True)
                acc_scr[h] = alpha * acc_scr[h] + jnp.dot(p.astype(bf16), v, preferred_element_type=f32)
                m_scr[h] = m_new

    @pl.when(j < i)
    def _():
        tile(False)

    @pl.when(j == i)
    def _():
        tile(True)

    @pl.when(j == nk - 1)
    def _():
        for h in range(NH * G):
            o_ref[:, h * DV:(h + 1) * DV] = (acc_scr[h] / l_scr[h]).astype(o_ref.dtype)


def _flash_prompt(q, k, v, n_kv, NH, G, DQ, DV, fq=None, fk=None):
    nq = SEQ // FL_T
    has_bias = fq is not None
    qmap = lambda b, h, i, j: (b * nq + i, h)
    kmap = lambda b, h, i, j: (b * nq + jnp.minimum(i, j), h)
    in_specs = [pl.BlockSpec((FL_T, NH * G * DQ), qmap), pl.BlockSpec((FL_T, NH * DQ), kmap),
                pl.BlockSpec((FL_T, NH * DV), kmap)]
    ins = [q, k, v]
    if has_bias:
        assert NH * G == H_B
        in_specs += [pl.BlockSpec((FL_T, H_B), lambda b, h, i, j: (b * nq + i, 0)),
                     pl.BlockSpec((None, H_B, FL_T), lambda b, h, i, j: (b, 0, jnp.minimum(i, j)))]
        ins += [fq, fk]
    nh = NH * G
    return pl.pallas_call(
        functools.partial(_flash_body, NH=NH, G=G, DQ=DQ, DV=DV, has_bias=has_bias, nk=nq),
        grid=(BATCH, n_kv // NH, nq, nq),
        in_specs=in_specs,
        out_specs=pl.BlockSpec((FL_T, nh * DV), qmap),
        out_shape=jax.ShapeDtypeStruct((MP, n_kv * G * DV), bf16),
        scratch_shapes=[pltpu.VMEM((nh, FL_T, 1), f32), pltpu.VMEM((nh, FL_T, 1), f32), pltpu.VMEM((nh, FL_T, DV), f32)],
        compiler_params=_cparams(("parallel", "parallel", "parallel", "arbitrary")),
        name="flash_prompt_bias" if has_bias else "flash_prompt",
    )(*ins)


def _page_copies(pt_ref, layer, b, c, slot, pages, specs):
    out = []
    for g in range(pages):
        page = pt_ref[b, c * pages + g]
        for hbm_ref, buf_ref, sem_ref, rows in specs:
            if rows:
                dst = buf_ref.at[slot, pl.ds(g * rows, rows), :]
            else:
                dst = buf_ref.at[slot, :, pl.ds(g * PAGE_SIZE, PAGE_SIZE)]
            out.append(pltpu.make_async_copy(hbm_ref.at[layer, page], dst, sem_ref.at[slot]))
    return out


def _gather_pipeline(pt_ref, layer, n_chunks, pages, specs):
    b, c = pl.program_id(0), pl.program_id(1)
    n = b * n_chunks + c
    total = pl.num_programs(0) * n_chunks
    slot = n % 2

    @pl.when(n == 0)
    def _():
        for cp in _page_copies(pt_ref, layer, 0, 0, 0, pages, specs):
            cp.start()

    nxt = n + 1

    @pl.when(nxt < total)
    def _():
        for cp in _page_copies(pt_ref, layer, nxt // n_chunks, nxt % n_chunks, nxt % 2, pages, specs):
            cp.start()

    for cp in _page_copies(pt_ref, layer, b, c, slot, pages, specs):
        cp.wait()
    return slot


def _softmax_update(s, v, m_ref, l_ref, acc_ref):
    m_prev = m_ref[...]
    m_new = jnp.maximum(m_prev, jnp.max(s, axis=-1, keepdims=True))
    alpha = jnp.exp(m_prev - m_new)
    p = jnp.exp(s - m_new)
    l_ref[...] = alpha * l_ref[...] + jnp.sum(p, axis=-1, keepdims=True)
    acc_ref[...] = alpha * acc_ref[...] + jnp.dot(p.astype(bf16), v, preferred_element_type=f32)
    m_ref[...] = m_new


def _new_key_mask(rows):
    t = lax.broadcasted_iota(i32, (rows, LANES), 0) % DEC_SEQ
    col = lax.broadcasted_iota(i32, (rows, LANES), 1)
    return (col < DEC_SEQ) & (col <= t)


LF_ROWS = PAST_LEN + CS_CHUNK
LF_SB = 16


def _lf_copies(pt_ref, cache_ref, buf, sem, layer, step, slot, start):
    def one_seq(sl, carry):
        for p in range(N_PAGES):
            cp = pltpu.make_async_copy(cache_ref.at[layer, pt_ref[step * LF_SB + sl, p]],
                                       buf.at[slot, pl.ds(sl * H_B, H_B), pl.ds(p * PAGE_SIZE, PAGE_SIZE)],
                                       sem.at[slot])
            if start:
                cp.start()
            else:
                cp.wait()
        return carry
    lax.fori_loop(0, LF_SB, one_seq, 0)


def _lf_sample_body(pt_ref, cache_ref, lfn_ref, fk_ref, fn_ref, buf, sem, *, layer):
    n = pl.program_id(0)
    slot = n % 2

    @pl.when(n == 0)
    def _():
        _lf_copies(pt_ref, cache_ref, buf, sem, layer, 0, 0, True)

    @pl.when(n + 1 < pl.num_programs(0))
    def _():
        _lf_copies(pt_ref, cache_ref, buf, sem, layer, n + 1, (n + 1) % 2, True)

    _lf_copies(pt_ref, cache_ref, buf, sem, layer, n, slot, False)
    buf[slot, :, PAST_LEN:LF_ROWS] = lfn_ref[...]

    def parts_of(ch):
        return _split3(buf[slot, :, ch * CS_CHUNK:(ch + 1) * CS_CHUNK])

    def write(ch, cs):
        if ch < PAST_LEN // CS_CHUNK:
            fk_ref[:, ch * CS_CHUNK:(ch + 1) * CS_CHUNK] = cs
        else:
            fn_ref[...] = cs[:, 0:LANES]

    _cumsum_chunks(parts_of, LF_ROWS // CS_CHUNK, write, jnp.zeros((LF_SB * H_B, 1), f32))


def _lf_sample(page_table, cache_logf_t, lf_new_t, layer):
    rows = LF_SB * H_B
    grid_spec = pltpu.PrefetchScalarGridSpec(
        num_scalar_prefetch=1,
        grid=(DEC_BATCH // LF_SB,),
        in_specs=[pl.BlockSpec(memory_space=pl.ANY),
                  pl.BlockSpec((rows, CS_CHUNK), lambda n, pt: (n, 0))],
        out_specs=[pl.BlockSpec((rows, PAST_LEN), lambda n, pt: (n, 0)),
                   pl.BlockSpec((rows, LANES), lambda n, pt: (n, 0))],
        scratch_shapes=[pltpu.VMEM((2, rows, LF_ROWS), f32), pltpu.SemaphoreType.DMA((2,))],
    )
    return pl.pallas_call(
        functools.partial(_lf_sample_body, layer=layer),
        grid_spec=grid_spec,
        out_shape=[jax.ShapeDtypeStruct((DEC_BATCH * H_B, PAST_LEN), f32),
                   jax.ShapeDtypeStruct((DEC_BATCH * H_B, LANES), f32)],
        compiler_params=_cparams(("arbitrary",)),
        name="fox_logf_sample",
    )(page_table, cache_logf_t, lf_new_t)


MLA_PAGES = 16
MLA_KEYS = MLA_PAGES * PAGE_SIZE
MLA_CHUNKS = N_PAGES // MLA_PAGES
DEC_SUB = MXU_DIM
QROWS = H_A * DEC_SEQ


def _mla_scores(a_ref, c_bf, rope):
    r = lax.dot_general(a_ref[...], c_bf, NT, preferred_element_type=f32)
    out = []
    for h in range(H_A):
        kk = r[h * NOPE_D:(h + 1) * NOPE_D, :]
        inv = lax.rsqrt(jnp.sum(kk * kk, axis=0, keepdims=True) / NOPE_D + EPS)
        lo_, hi_ = H_A * NOPE_D + h * DEC_SEQ, H_A * NOPE_D + (h + 1) * DEC_SEQ
        out.append(r[lo_:hi_, :] * inv)
    return jnp.concatenate(out, axis=0) + rope


def _mla_decode_body(pt_ref, ckv_hbm, kr_hbm, q_ref, cn_ref, krn_ref, wukt_ref, wuv_ref, gkn_ref, o_ref,
                     cbuf, krbuf, csem, krsem, a_scr, qr_scr, s_scr, cn_pad, krn_pad, m_scr, l_scr, acc_scr, *, layer):
    b, c = pl.program_id(0), pl.program_id(1)
    nk = H_A * NOPE_D

    @pl.when((b == 0) & (c == 0))
    def _():
        a_scr[0:nk, :] = wukt_ref[...]
        cn_pad[...] = jnp.zeros(cn_pad.shape, f32)
        krn_pad[...] = jnp.zeros(krn_pad.shape, f32)

    slot = _gather_pipeline(pt_ref, layer, MLA_CHUNKS, MLA_PAGES,
                            [(ckv_hbm, cbuf, csem, PAGE_SIZE), (kr_hbm, krbuf, krsem, 0)])

    @pl.when(c == 0)
    def _():
        q = q_ref[...]
        wq, qr_rows = [], []
        for h in range(H_A):
            qg = (q[:, h * QH:h * QH + NOPE_D] * gkn_ref[...]).astype(bf16)
            wq.append(jnp.dot(qg, wukt_ref[h * NOPE_D:(h + 1) * NOPE_D, :], preferred_element_type=f32))
            qr_rows.append(q[:, h * QH + NOPE_D:h * QH + NOPE_D + ROPE_D])
        a_scr[nk:nk + QROWS, :] = jnp.concatenate(wq, axis=0).astype(bf16)
        qr_scr[...] = jnp.concatenate(qr_rows, axis=0).astype(bf16)
        m_scr[...] = jnp.full(m_scr.shape, NEG_INF, f32)
        l_scr[...] = jnp.zeros(l_scr.shape, f32)
        acc_scr[...] = jnp.zeros(acc_scr.shape, f32)

    qr = qr_scr[...]
    for sub in range(MLA_KEYS // DEC_SUB):
        keys = pl.ds(sub * DEC_SUB, DEC_SUB)
        rope = jnp.dot(qr, krbuf[slot, :, keys].astype(bf16), preferred_element_type=f32)
        s_scr[:, sub * DEC_SUB:(sub + 1) * DEC_SUB] = _mla_scores(a_scr, cbuf[slot, keys, :].astype(bf16), rope)
    _softmax_update(s_scr[...], cbuf[slot].astype(bf16), m_scr, l_scr, acc_scr)

    @pl.when(c == MLA_CHUNKS - 1)
    def _():
        cn_pad[0:DEC_SEQ, :] = cn_ref[...]
        krn_pad[0:DEC_SEQ, :] = krn_ref[...]
        cn = cn_pad[...].astype(bf16)
        rope = lax.dot_general(qr, krn_pad[...].astype(bf16), NT, preferred_element_type=f32)
        s = _mla_scores(a_scr, cn, rope)
        s = jnp.where(_new_key_mask(QROWS), s, NEG_INF)
        _softmax_update(s, cn, m_scr, l_scr, acc_scr)
        o_lat = (acc_scr[...] / l_scr[...]).astype(bf16)
        for h in range(H_A):
            o_ref[:, h * V_D:(h + 1) * V_D] = jnp.dot(
                o_lat[h * DEC_SEQ:(h + 1) * DEC_SEQ, :], wuv_ref[:, h * V_D:(h + 1) * V_D],
                preferred_element_type=f32).astype(o_ref.dtype)


def _mla_decode(page_table, cache_ckv, cache_krope, q_s, c_new, kr_new, wukt, wuv, gkn, layer):
    seq = lambda w: pl.BlockSpec((None, DEC_SEQ, w), lambda b, c, pt: (b, 0, 0))
    full = lambda a: pl.BlockSpec(a.shape, lambda b, c, pt: (0,) * a.ndim)
    grid_spec = pltpu.PrefetchScalarGridSpec(
        num_scalar_prefetch=1,
        grid=(DEC_BATCH, MLA_CHUNKS),
        in_specs=[pl.BlockSpec(memory_space=pl.ANY), pl.BlockSpec(memory_space=pl.ANY),
                  seq(H_A * QH), seq(KV_LORA), seq(ROPE_D), full(wukt), full(wuv), full(gkn)],
        out_specs=seq(H_A * V_D),
        scratch_shapes=[
            pltpu.VMEM((2, MLA_KEYS, KV_LORA), f32), pltpu.VMEM((2, ROPE_D, MLA_KEYS), f32),
            pltpu.SemaphoreType.DMA((2,)), pltpu.SemaphoreType.DMA((2,)),
            pltpu.VMEM((H_A * NOPE_D + QROWS, KV_LORA), bf16), pltpu.VMEM((QROWS, ROPE_D), bf16),
            pltpu.VMEM((QROWS, MLA_KEYS), f32),
            pltpu.VMEM((LANES, KV_LORA), f32), pltpu.VMEM((LANES, ROPE_D), f32),
            pltpu.VMEM((QROWS, 1), f32), pltpu.VMEM((QROWS, 1), f32), pltpu.VMEM((QROWS, KV_LORA), f32),
        ],
    )
    return pl.pallas_call(
        functools.partial(_mla_decode_body, layer=layer),
        grid_spec=grid_spec,
        out_shape=jax.ShapeDtypeStruct((DEC_BATCH, DEC_SEQ, H_A * V_D), f32),
        compiler_params=_cparams(("arbitrary", "arbitrary")),
        name="mla_decode",
    )(page_table, cache_ckv, cache_krope, q_s, c_new, kr_new, wukt, wuv, gkn)


GROWS = G_B * DEC_SEQ
FOX_PAGES = 32
FOX_KEYS = FOX_PAGES * PAGE_SIZE
FOX_CHUNKS = N_PAGES // FOX_PAGES


def _fox_decode_body(pt_ref, k_hbm, v_hbm, q_ref, kn_ref, vn_ref, fq_ref, fk_ref, fn_ref, o_ref,
                     kbuf, vbuf, ksem, vsem, kn_pad, vn_pad, q_scr, m_scr, l_scr, acc_scr, *, layer):
    b, c = pl.program_id(0), pl.program_id(1)

    @pl.when((b == 0) & (c == 0))
    def _():
        kn_pad[...] = jnp.zeros(kn_pad.shape, f32)
        vn_pad[...] = jnp.zeros(vn_pad.shape, f32)

    slot = _gather_pipeline(pt_ref, layer, FOX_CHUNKS, FOX_PAGES,
                            [(k_hbm, kbuf, ksem, PAGE_SIZE * KV_B), (v_hbm, vbuf, vsem, PAGE_SIZE * KV_B)])

    @pl.when(c == 0)
    def _():
        q = q_ref[...]
        q_scr[...] = jnp.concatenate([q[:, h * HD_B:(h + 1) * HD_B] for h in range(H_B)], axis=0).astype(bf16)
        m_scr[...] = jnp.full(m_scr.shape, NEG_INF, f32)
        l_scr[...] = jnp.zeros(l_scr.shape, f32)
        acc_scr[...] = jnp.zeros(acc_scr.shape, f32)

    def bias(fk):
        rows = [fq_ref[h * DEC_SEQ:(h + 1) * DEC_SEQ, :] - fk[h:h + 1, :] for h in range(H_B)]
        return jnp.concatenate(rows, axis=0)

    def attend(kv, k_bf, v_bf, bias_kv, mask):
        rows = slice(kv * GROWS, (kv + 1) * GROWS)
        s = lax.dot_general(q_scr[rows, :], k_bf, NT, preferred_element_type=f32) + bias_kv
        if mask is not None:
            s = jnp.where(mask, s, NEG_INF)
        _softmax_update(s, v_bf, m_scr.at[rows], l_scr.at[rows], acc_scr.at[rows])

    bias_past = bias(fk_ref[...])
    for kv in range(KV_B):
        sel = pl.ds(kv, FOX_KEYS, stride=KV_B)
        attend(kv, kbuf[slot, sel, :].astype(bf16), vbuf[slot, sel, :].astype(bf16),
               bias_past[kv * GROWS:(kv + 1) * GROWS], None)

    @pl.when(c == FOX_CHUNKS - 1)
    def _():
        bias_new = bias(fn_ref[...])
        mask = _new_key_mask(GROWS)
        for kv in range(KV_B):
            kn_pad[0:DEC_SEQ, :] = kn_ref[:, kv * HD_B:(kv + 1) * HD_B]
            vn_pad[0:DEC_SEQ, :] = vn_ref[:, kv * HD_B:(kv + 1) * HD_B]
            attend(kv, kn_pad[...].astype(bf16), vn_pad[...].astype(bf16),
                   bias_new[kv * GROWS:(kv + 1) * GROWS], mask)
        o = acc_scr[...] / l_scr[...]
        for h in range(H_B):
            o_ref[:, h * HD_B:(h + 1) * HD_B] = o[h * DEC_SEQ:(h + 1) * DEC_SEQ, :].astype(o_ref.dtype)


def _fox_decode(page_table, cache_k, cache_v, q_s, k_new, v_new, fq_col, fk_past, fn_new, layer):
    seq = lambda w: pl.BlockSpec((None, DEC_SEQ, w), lambda b, c, pt: (b, 0, 0))
    grid_spec = pltpu.PrefetchScalarGridSpec(
        num_scalar_prefetch=1,
        grid=(DEC_BATCH, FOX_CHUNKS),
        in_specs=[pl.BlockSpec(memory_space=pl.ANY), pl.BlockSpec(memory_space=pl.ANY),
                  seq(H_B * HD_B), seq(KV_B * HD_B), seq(KV_B * HD_B),
                  pl.BlockSpec((None, H_B * DEC_SEQ, 1), lambda b, c, pt: (b, 0, 0)),
                  pl.BlockSpec((None, H_B, FOX_KEYS), lambda b, c, pt: (b, 0, c)),
                  pl.BlockSpec((None, H_B, LANES), lambda b, c, pt: (b, 0, 0))],
        out_specs=seq(H_B * HD_B),
        scratch_shapes=[
            pltpu.VMEM((2, FOX_KEYS * KV_B, HD_B), f32), pltpu.VMEM((2, FOX_KEYS * KV_B, HD_B), f32),
            pltpu.SemaphoreType.DMA((2,)), pltpu.SemaphoreType.DMA((2,)),
            pltpu.VMEM((LANES, HD_B), f32), pltpu.VMEM((LANES, HD_B), f32),
            pltpu.VMEM((H_B * DEC_SEQ, HD_B), bf16),
            pltpu.VMEM((H_B * DEC_SEQ, 1), f32), pltpu.VMEM((H_B * DEC_SEQ, 1), f32),
            pltpu.VMEM((H_B * DEC_SEQ, HD_B), f32),
        ],
    )
    return pl.pallas_call(
        functools.partial(_fox_decode_body, layer=layer),
        grid_spec=grid_spec,
        out_shape=jax.ShapeDtypeStruct((DEC_BATCH, DEC_SEQ, H_B * HD_B), f32),
        compiler_params=_cparams(("arbitrary", "arbitrary")),
        name="fox_decode",
    )(page_table, cache_k, cache_v, q_s, k_new, v_new, fq_col, fk_past, fn_new)


C_TM = 256
C_IN = (H_C + 2 * KV_C) * HD_C


def _norm_pair_tile(t, g128, lo):
    ss = t * t
    s_lo = jnp.sum(jnp.where(lo, ss, 0.0), axis=-1, keepdims=True)
    s_hi = jnp.sum(jnp.where(lo, 0.0, ss), axis=-1, keepdims=True)
    r = jnp.where(lo, lax.rsqrt(s_lo / 64.0 + EPS), lax.rsqrt(s_hi / 64.0 + EPS))
    return (t * r) * g128


def _cproj_body(x_ref, mg_ref, win_ref, gq_ref, gk_ref, q_ref, k_ref, v_ref):
    n = _rms(x_ref[...], mg_ref[...]).astype(bf16)
    y = jnp.dot(n, win_ref[...], preferred_element_type=f32)
    lo, _ = _half_masks((C_TM, LANES))
    nq = H_C * HD_C
    for p in range(nq // LANES):
        q_ref[:, p * LANES:(p + 1) * LANES] = (
            _norm_pair_tile(y[:, p * LANES:(p + 1) * LANES], gq_ref[...], lo) * C_SCALE).astype(bf16)
    for p in range(KV_C * HD_C // LANES):
        k_ref[:, p * LANES:(p + 1) * LANES] = _norm_pair_tile(y[:, nq + p * LANES:nq + (p + 1) * LANES], gk_ref[...], lo)
    v_ref[...] = y[:, nq + KV_C * HD_C:]


def _cproj(x, mg, win, gq128, gk128):
    m = x.shape[0]
    row = lambda w: pl.BlockSpec((C_TM, w), lambda i: (i, 0))
    full = lambda a: pl.BlockSpec(a.shape, lambda i: (0,) * a.ndim)
    return pl.pallas_call(
        _cproj_body,
        grid=(m // C_TM,),
        in_specs=[row(D_MODEL), full(mg), full(win), full(gq128), full(gk128)],
        out_specs=[row(H_C * HD_C), row(KV_C * HD_C), row(KV_C * HD_C)],
        out_shape=[jax.ShapeDtypeStruct((m, H_C * HD_C), bf16), jax.ShapeDtypeStruct((m, KV_C * HD_C), f32),
                   jax.ShapeDtypeStruct((m, KV_C * HD_C), f32)],
        compiler_params=_cparams(("parallel",)),
        name="swa_proj",
    )(x, mg, win, gq128, gk128)


def _t5_bucket_np(dist):
    n = np.maximum(dist, 0)
    nf = np.maximum(n, 1).astype(np.float32)
    large = MAX_EXACT + (np.log(nf / np.float32(MAX_EXACT)) / np.float32(math.log(MAX_DISTANCE / MAX_EXACT))
                         * np.float32(NUM_BUCKETS - MAX_EXACT)).astype(np.int32)
    return np.where(n < MAX_EXACT, n, np.minimum(large, NUM_BUCKETS - 1)).astype(np.int32)


def _fill_bias(rel_ref, bucket, bias_scr):
    def one_head(h, carry):
        acc = jnp.zeros(bucket.shape, f32)
        for bk in range(NUM_BUCKETS):
            acc = jnp.where(bucket == bk, rel_ref[bk, h], acc)
        bias_scr[h] = acc
        return carry
    lax.fori_loop(0, H_C, one_head, 0)


def _sink_softmax_pv(s, sink, v_bf):
    m = jnp.maximum(jnp.max(s, axis=-1, keepdims=True), sink)
    e = jnp.exp(s - m)
    den = jnp.sum(e, axis=-1, keepdims=True) + jnp.exp(sink - m)
    return jnp.dot((e / den).astype(bf16), v_bf, preferred_element_type=f32)


def _swa_prompt_body(rel_ref, sink_ref, bucket_ref, q_ref, kp_ref, kc_ref, vp_ref, vc_ref, o_ref, bias_scr):
    b, n = pl.program_id(0), pl.program_id(1)

    @pl.when((b == 0) & (n == 0))
    def _():
        _fill_bias(rel_ref, bucket_ref[...], bias_scr)

    kb = jnp.concatenate([kp_ref[...], kc_ref[...]], axis=0).astype(bf16)
    vb = jnp.concatenate([vp_ref[...], vc_ref[...]], axis=0).astype(bf16)
    row = lax.broadcasted_iota(i32, (WINDOW, 2 * WINDOW), 0) + WINDOW
    col = lax.broadcasted_iota(i32, (WINDOW, 2 * WINDOW), 1)
    dist = row - col
    mask = (dist >= 0) & (dist < WINDOW) & ((n > 0) | (col >= WINDOW))
    for kv in range(KV_C):
        k_h = kb[:, kv * HD_C:(kv + 1) * HD_C]
        v_h = vb[:, kv * HD_C:(kv + 1) * HD_C]
        outs = []
        for g in range(G_C):
            h = kv * G_C + g
            s = lax.dot_general(q_ref[:, h * HD_C:(h + 1) * HD_C], k_h, NT, preferred_element_type=f32) + bias_scr[h]
            s = jnp.where(mask, s, NEG_INF)
            outs.append(_sink_softmax_pv(s, sink_ref[h], v_h))
        for p in range(G_C // 2):
            tile = jnp.concatenate([outs[2 * p], outs[2 * p + 1]], axis=1)
            col0 = (kv * G_C + 2 * p) * HD_C
            o_ref[:, col0:col0 + LANES] = tile.astype(o_ref.dtype)


def _swa_prompt(q, k, v, rel_bias, sinks):
    nb = SEQ // WINDOW
    bucket = jnp.asarray(_t5_bucket_np(WINDOW + np.arange(WINDOW)[:, None] - np.arange(2 * WINDOW)[None, :]))
    smem = pl.BlockSpec(memory_space=pltpu.SMEM)
    cur = lambda w: pl.BlockSpec((WINDOW, w), lambda b, n: (b * nb + n, 0))
    prev = lambda w: pl.BlockSpec((WINDOW, w), lambda b, n: (b * nb + jnp.maximum(n - 1, 0), 0))
    kvw = KV_C * HD_C
    return pl.pallas_call(
        _swa_prompt_body,
        grid=(BATCH, nb),
        in_specs=[smem, smem, pl.BlockSpec(bucket.shape, lambda b, n: (0, 0)),
                  cur(H_C * HD_C), prev(kvw), cur(kvw), prev(kvw), cur(kvw)],
        out_specs=cur(H_C * HD_C),
        out_shape=jax.ShapeDtypeStruct((MP, H_C * HD_C), bf16),
        scratch_shapes=[pltpu.VMEM((H_C, WINDOW, 2 * WINDOW), f32)],
        compiler_params=_cparams(("arbitrary", "arbitrary")),
        name="swa_prompt",
    )(rel_bias, sinks, bucket, q, k, k, v, v)


CROWS = G_C * DEC_SEQ


def _swa_decode_body(rel_ref, sink_ref, bucket_ref, q_ref, kb_ref, vb_ref, kn_ref, vn_ref,
                     o_ref, ko_ref, vo_ref, bias_scr, kn_pad, vn_pad):
    b = pl.program_id(0)

    @pl.when(b == 0)
    def _():
        _fill_bias(rel_ref, bucket_ref[...], bias_scr)
        kn_pad[...] = jnp.zeros(kn_pad.shape, f32)
        vn_pad[...] = jnp.zeros(vn_pad.shape, f32)

    kn_pad[0:DEC_SEQ, :] = kn_ref[...]
    vn_pad[0:DEC_SEQ, :] = vn_ref[...]
    keys = jnp.concatenate([kb_ref[...], kn_pad[...]], axis=0).astype(bf16)
    vals = jnp.concatenate([vb_ref[...], vn_pad[...]], axis=0).astype(bf16)
    q = q_ref[...]

    t = lax.broadcasted_iota(i32, (DEC_SEQ, 2 * WINDOW), 0)
    col = lax.broadcasted_iota(i32, (DEC_SEQ, 2 * WINDOW), 1)
    dist = WINDOW + t - col
    mask = (dist >= 0) & (dist < WINDOW)
    for kv in range(KV_C):
        k_h = keys[:, kv * HD_C:(kv + 1) * HD_C]
        v_h = vals[:, kv * HD_C:(kv + 1) * HD_C]
        q_kv = jnp.concatenate([q[:, (kv * G_C + g) * HD_C:(kv * G_C + g + 1) * HD_C] for g in range(G_C)],
                               axis=0).astype(bf16)
        s_all = lax.dot_general(q_kv, k_h, NT, preferred_element_type=f32)
        outs = []
        for g in range(G_C):
            h = kv * G_C + g
            s = s_all[g * DEC_SEQ:(g + 1) * DEC_SEQ, :] + bias_scr[h]
            s = jnp.where(mask, s, NEG_INF)
            outs.append(_sink_softmax_pv(s, sink_ref[h], v_h))
        for p in range(G_C // 2):
            tile = jnp.concatenate([outs[2 * p], outs[2 * p + 1]], axis=1)
            col0 = (kv * G_C + 2 * p) * HD_C
            o_ref[:, col0:col0 + LANES] = tile.astype(o_ref.dtype)

    keep = WINDOW - DEC_SEQ
    ko_ref[0:keep, :] = kb_ref[DEC_SEQ:WINDOW, :]
    ko_ref[keep:WINDOW, :] = kn_ref[...]
    vo_ref[0:keep, :] = vb_ref[DEC_SEQ:WINDOW, :]
    vo_ref[keep:WINDOW, :] = vn_ref[...]


def _swa_decode(q_s, buf_k, buf_v, k_new, v_new, rel_bias, sinks):
    bucket = jnp.asarray(_t5_bucket_np(WINDOW + np.arange(DEC_SEQ)[:, None] - np.arange(2 * WINDOW)[None, :]))
    smem = pl.BlockSpec(memory_space=pltpu.SMEM)
    kvw = KV_C * HD_C
    seq = lambda r, w: pl.BlockSpec((None, r, w), lambda b: (b, 0, 0))
    return pl.pallas_call(
        _swa_decode_body,
        grid=(DEC_BATCH,),
        in_specs=[smem, smem, pl.BlockSpec(bucket.shape, lambda b: (0, 0)),
                  seq(DEC_SEQ, H_C * HD_C), seq(WINDOW, kvw), seq(WINDOW, kvw), seq(DEC_SEQ, kvw), seq(DEC_SEQ, kvw)],
        out_specs=[seq(DEC_SEQ, H_C * HD_C), seq(WINDOW, kvw), seq(WINDOW, kvw)],
        out_shape=[jax.ShapeDtypeStruct((DEC_BATCH, DEC_SEQ, H_C * HD_C), f32),
                   jax.ShapeDtypeStruct((DEC_BATCH, WINDOW, kvw), f32),
                   jax.ShapeDtypeStruct((DEC_BATCH, WINDOW, kvw), f32)],
        scratch_shapes=[pltpu.VMEM((H_C, DEC_SEQ, 2 * WINDOW), f32), pltpu.VMEM((WINDOW, kvw), f32),
                        pltpu.VMEM((WINDOW, kvw), f32)],
        compiler_params=_cparams(("arbitrary",)),
        name="swa_decode",
    )(rel_bias, sinks, bucket, q_s, buf_k, buf_v, k_new, v_new)


def _rope_tables():
    half = ROPE_D // 2
    inv = ROPE_THETA ** (-jnp.arange(half, dtype=f32) / half)
    pos = jnp.concatenate([jnp.tile(jnp.arange(SEQ, dtype=i32), BATCH),
                           jnp.tile(PAST_LEN + jnp.arange(DEC_SEQ, dtype=i32), DEC_BATCH)])
    ang = pos.astype(f32)[:, None] * inv[None, :]
    cos, sin = jnp.cos(ang), jnp.sin(ang)
    return jnp.tile(cos, (1, 4)), jnp.tile(jnp.concatenate([-sin, sin], axis=1), (1, 2))


def _row(v, reps=1):
    return jnp.tile(v.astype(f32), reps).reshape(1, -1)


def _ab_layer(x, j, tabs, caches, page_table, p):
    cos128, sin128 = tabs
    cache_ckv, cache_krope, cache_k, cache_v, cache_logf = caches
    w_in = p["ab_w_in"][j]
    pad = jnp.zeros((D_MODEL, IN_AB_PAD - w_in.shape[1]), f32)
    win = jnp.concatenate([w_in[:, 0:1024], w_in[:, 1088:2624], w_in[:, 1024:1088], w_in[:, 2624:2632], pad],
                          axis=1).astype(bf16)
    wq = p["mla_w_q_up"][j].reshape(Q_LORA, H_A, NOPE_D + ROPE_D)
    wqu = jnp.concatenate([wq[:, :, :NOPE_D].reshape(Q_LORA, -1), wq[:, :, NOPE_D:].reshape(Q_LORA, -1)],
                          axis=1).astype(bf16)
    bf128 = jnp.zeros((1, LANES), f32).at[0, 64:64 + H_B].set(p["fox_b_f"][j])
    gkr128 = jnp.concatenate([p["mla_g_kr"][j], jnp.ones((64,), f32)]).reshape(1, LANES)
    qmla, ckv, krope, kr128, fq, fk, fv, logf = _abproj(
        x, _row(p["mix_g"][2 * j]), win, _row(p["mla_g_q_lat"][j]), wqu, _row(p["mla_g_kv_lat"][j]),
        _row(p["mla_g_qn"][j]), _row(p["mla_g_qr"][j], 2), gkr128, _row(p["fox_g_q"][j]), _row(p["fox_g_k"][j]),
        bf128, cos128, sin128)
    wuk = p["mla_w_uk"][j].astype(bf16)
    wuv = p["mla_w_uv"][j].astype(bf16)
    gkn = _row(p["mla_g_kn"][j])
    proj = (qmla, ckv, krope, kr128, fq, fk, fv, logf)
    o_p = _ab_prompt_attn(proj, wuk, wuv, gkn)
    o_s = _ab_sample_attn(proj, caches, page_table, wuk, wuv, gkn, j)
    x = _outproj(x, jnp.concatenate([o_p, o_s], axis=0), p["ab_w_out"][j].astype(bf16))
    seqs = lambda a: a[MP:].reshape(DEC_BATCH, DEC_SEQ, -1)
    st_p = (ckv[:MP].reshape(BATCH, SEQ, KV_LORA), krope[:MP].reshape(BATCH, SEQ, ROPE_D),
            fk[:MP].reshape(BATCH, SEQ, KV_B, HD_B), fv[:MP].reshape(BATCH, SEQ, KV_B, HD_B),
            logf[:MP].reshape(BATCH, SEQ, H_B))
    st_s = (seqs(ckv), seqs(krope), seqs(fk).reshape(DEC_BATCH, DEC_SEQ, KV_B, HD_B),
            seqs(fv).reshape(DEC_BATCH, DEC_SEQ, KV_B, HD_B), seqs(logf))
    return x, st_p, st_s


def _ab_prompt_attn(proj, wuk, wuv, gkn):
    qmla, ckv, krope, kr128, fq, fk, fv, logf = proj
    kmla, va = _kvup(ckv, kr128, wuk, wuv, gkn)
    o_a = _flash_prompt(qmla, kmla, va, H_A, 2, 1, QH, V_D)
    lf_p = logf[:MP].reshape(BATCH, SEQ, H_B)
    f_t = _cumsum_prompt(jnp.swapaxes(lf_p, 1, 2))
    f_tok = jnp.swapaxes(f_t, 1, 2).reshape(MP, H_B)
    o_b = _flash_prompt(fq, fk, fv, KV_B, KV_B, G_B, HD_B, HD_B, fq=f_tok, fk=f_t)
    return jnp.concatenate([o_a, o_b], axis=1)


def _ab_sample_attn(proj, caches, page_table, wuk, wuv, gkn, j):
    qmla, ckv, krope, kr128, fq, fk, fv, logf = proj
    cache_ckv, cache_krope_t, cache_k, cache_v, cache_logf_t = caches
    seqs = lambda a: a[MP:].reshape(DEC_BATCH, DEC_SEQ, -1)
    lf_new_t = jnp.swapaxes(seqs(logf), 1, 2).reshape(DEC_BATCH * H_B, DEC_SEQ)
    lf_new_t = jnp.pad(lf_new_t, ((0, 0), (0, CS_CHUNK - DEC_SEQ)))
    fk_past, fn_new = _lf_sample(page_table, cache_logf_t, lf_new_t, j)
    fk_past = fk_past.reshape(DEC_BATCH, H_B, PAST_LEN)
    fn_new = fn_new.reshape(DEC_BATCH, H_B, LANES)
    fq_col = fn_new[:, :, :DEC_SEQ].reshape(DEC_BATCH, H_B * DEC_SEQ, 1)
    os_a = _mla_decode(page_table, cache_ckv, cache_krope_t, seqs(qmla).astype(f32), seqs(ckv), seqs(krope),
                       wuk.T, wuv, gkn, j)
    os_b = _fox_decode(page_table, cache_k, cache_v, seqs(fq).astype(f32), seqs(fk), seqs(fv),
                       fq_col, fk_past, fn_new, j)
    return jnp.concatenate([os_a, os_b], axis=2).reshape(MS, D_MODEL).astype(bf16)


def _c_layer(x, j, state_k, state_v, p):
    kvw = KV_C * HD_C
    q, k, v = _cproj(x, _row(p["mix_g"][2 * j + 1]), p["swa_w_in"][j].astype(bf16),
                     _row(p["swa_g_q"][j], 2), _row(p["swa_g_k"][j], 2))
    rel = p["rel_bias"].astype(f32)
    sinks = p["swa_sinks"][j].astype(f32)
    o_p = _swa_prompt(q, k, v, rel, sinks)
    seqs = lambda a: a[MP:].reshape(DEC_BATCH, DEC_SEQ, -1)
    o_s, k_out, v_out = _swa_decode(seqs(q).astype(f32), state_k[j].reshape(DEC_BATCH, WINDOW, kvw),
                                    state_v[j].reshape(DEC_BATCH, WINDOW, kvw), seqs(k), seqs(v), rel, sinks)
    o = jnp.concatenate([o_p, o_s.reshape(MS, D_MODEL).astype(bf16)], axis=0)
    x = _outproj(x, o, p["swa_w_out"][j].astype(bf16))
    tail = lambda a: a[:MP].reshape(BATCH, SEQ, KV_C, HD_C)[:, SEQ - WINDOW:]
    st_p = (tail(k), tail(v))
    st_s = (k_out.reshape(DEC_BATCH, WINDOW, KV_C, HD_C), v_out.reshape(DEC_BATCH, WINDOW, KV_C, HD_C))
    return x, st_p, st_s


def kernel(x_prompt, x_sample, cache_mla_ckv, cache_mla_krope, cache_fox_k, cache_fox_v, cache_fox_logf, state_swa_k, state_swa_v, page_table, ffn1_g, ffn1_w_gate, ffn1_w_up, ffn1_w_down, mix_g, ffn2_g, ffn2_w_gate, ffn2_w_up, ffn2_w_down, ab_w_in, mla_g_q_lat, mla_w_q_up, mla_g_kv_lat, mla_w_uk, mla_w_uv, mla_g_qn, mla_g_qr, mla_g_kn, mla_g_kr, fox_g_q, fox_g_k, fox_b_f, ab_w_out, swa_w_in, swa_g_q, swa_g_k, swa_sinks, swa_w_out, rel_bias):
    assert WINDOW == PAGE_SIZE == LANES and min(WINDOW, PAST_LEN) == WINDOW
    p = dict(mix_g=mix_g, ab_w_in=ab_w_in, mla_g_q_lat=mla_g_q_lat, mla_w_q_up=mla_w_q_up,
             mla_g_kv_lat=mla_g_kv_lat, mla_w_uk=mla_w_uk, mla_w_uv=mla_w_uv, mla_g_qn=mla_g_qn,
             mla_g_qr=mla_g_qr, mla_g_kn=mla_g_kn, mla_g_kr=mla_g_kr, fox_g_q=fox_g_q, fox_g_k=fox_g_k,
             fox_b_f=fox_b_f, ab_w_out=ab_w_out, swa_w_in=swa_w_in, swa_g_q=swa_g_q, swa_g_k=swa_g_k,
             swa_sinks=swa_sinks, swa_w_out=swa_w_out, rel_bias=rel_bias)
    n_pool = cache_fox_k.shape[1]
    caches = (cache_mla_ckv, jnp.swapaxes(cache_mla_krope, 2, 3),
              cache_fox_k.reshape(-1, n_pool, PAGE_SIZE * KV_B, HD_B),
              cache_fox_v.reshape(-1, n_pool, PAGE_SIZE * KV_B, HD_B), jnp.swapaxes(cache_fox_logf, 2, 3))
    tabs = _rope_tables()
    x = jnp.concatenate([x_prompt.reshape(MP, D_MODEL), x_sample.reshape(MS, D_MODEL)], axis=0)
    ab_p, ab_s, c_p, c_s = [], [], [], []
    for l in range(DEPTH):
        x = _ffn(x, ffn1_g[l], ffn1_w_gate[l].astype(bf16), ffn1_w_up[l].astype(bf16), ffn1_w_down[l].astype(bf16))
        j = l // 2
        if l % 2 == 0:
            x, st_p, st_s = _ab_layer(x, j, tabs, caches, page_table, p)
            ab_p.append(st_p)
            ab_s.append(st_s)
        else:
            x, st_p, st_s = _c_layer(x, j, state_swa_k, state_swa_v, p)
            c_p.append(st_p)
            c_s.append(st_s)
        x = _ffn(x, ffn2_g[l], ffn2_w_gate[l].astype(bf16), ffn2_w_up[l].astype(bf16), ffn2_w_down[l].astype(bf16))

    def stack(states, i):
        return jnp.stack([st[i] for st in states], axis=0)

    return (x[:MP].reshape(BATCH, SEQ, D_MODEL), x[MP:].reshape(DEC_BATCH, DEC_SEQ, D_MODEL),
            stack(ab_p, 0), stack(ab_p, 1), stack(ab_p, 2), stack(ab_p, 3), stack(ab_p, 4),
            stack(c_p, 0), stack(c_p, 1),
            stack(ab_s, 0), stack(ab_s, 1), stack(ab_s, 2), stack(ab_s, 3), stack(ab_s, 4),
            stack(c_s, 0), stack(c_s, 1))
```

```python
import functools
import math

import numpy as np
import jax
import jax.numpy as jnp
from jax import lax
from jax.experimental import pallas as pl
from jax.experimental.pallas import tpu as pltpu

f32, bf16, i32 = jnp.float32, jnp.bfloat16, jnp.int32

D_MODEL = 2048
BATCH = 4
SEQ = 2048
DEPTH = 4
DEC_BATCH = 128
DEC_SEQ = 8
PAST_LEN = 8192
PAGE_SIZE = 128
N_PAGES = PAST_LEN // PAGE_SIZE
H_A = 8
Q_LORA = 512
KV_LORA = 512
NOPE_D = 128
ROPE_D = 64
V_D = 128
ROPE_THETA = 10000.0
H_B = 8
KV_B = 2
HD_B = 128
H_C = 32
KV_C = 4
HD_C = 64
WINDOW = 128
NUM_BUCKETS = 32
MAX_DISTANCE = 128
D_FF = 4096
FFN_RES = 0.5
EPS = 1e-6
NEG_INF = -1e30
MLA_SCALE = (NOPE_D + ROPE_D) ** -0.5
FOX_SCALE = HD_B ** -0.5
C_SCALE = HD_C ** -0.5
G_B = H_B // KV_B
G_C = H_C // KV_C
MAX_EXACT = NUM_BUCKETS // 2

MP = BATCH * SEQ
MS = DEC_BATCH * DEC_SEQ
MT = MP + MS
IN_AB_PAD = 2688
QH = 256

LANES = 128
MXU_DIM = 256
VMEM_LIMIT = 56 * 1024 * 1024

NT = (((1,), (1,)), ((), ()))


def _cparams(sem):
    return pltpu.CompilerParams(dimension_semantics=sem, vmem_limit_bytes=VMEM_LIMIT)


def _rms(x, g, eps=EPS):
    return (x * lax.rsqrt(jnp.mean(x * x, axis=-1, keepdims=True) + eps)) * g


def _split3(x):
    hi = x.astype(bf16)
    r1 = x - hi.astype(f32)
    mid = r1.astype(bf16)
    lo = (r1 - mid.astype(f32)).astype(bf16)
    return hi, mid, lo


FFN_TM = 1024
FFN_TF = 256


def _ffn_body(x_ref, g_ref, wg_ref, wu_ref, wd_ref, o_ref, n_ref):
    @pl.when(pl.program_id(1) == 0)
    def _():
        x = x_ref[...]
        n_ref[...] = _rms(x, g_ref[...]).astype(bf16)
        o_ref[...] = x

    n = n_ref[...]
    gate = jnp.dot(n, wg_ref[...].astype(bf16), preferred_element_type=f32)
    up = jnp.dot(n, wu_ref[...].astype(bf16), preferred_element_type=f32)
    h = (gate / (1.0 + jnp.exp(-gate))) * up * FFN_RES
    o_ref[...] += jnp.dot(h.astype(bf16), wd_ref[...].astype(bf16), preferred_element_type=f32)


def _ffn(x, g, wg, wu, wd, l):
    m = x.shape[0]
    return pl.pallas_call(
        _ffn_body,
        grid=(m // FFN_TM, D_FF // FFN_TF),
        in_specs=[
            pl.BlockSpec((FFN_TM, D_MODEL), lambda i, f: (i, 0), pipeline_mode=pl.Buffered(1)),
            pl.BlockSpec((1, D_MODEL), lambda i, f: (0, 0)),
            pl.BlockSpec((None, D_MODEL, FFN_TF), lambda i, f: (l, 0, f)),
            pl.BlockSpec((None, D_MODEL, FFN_TF), lambda i, f: (l, 0, f)),
            pl.BlockSpec((None, FFN_TF, D_MODEL), lambda i, f: (l, f, 0)),
        ],
        out_specs=pl.BlockSpec((FFN_TM, D_MODEL), lambda i, f: (i, 0)),
        out_shape=jax.ShapeDtypeStruct((m, D_MODEL), f32),
        scratch_shapes=[pltpu.VMEM((FFN_TM, D_MODEL), bf16)],
        compiler_params=_cparams(("parallel", "arbitrary")),
        name="ffn_half",
    )(x, g.reshape(1, D_MODEL), wg, wu, wd)


OUT_TM = 512


def _outproj_body(x_ref, o_ref, w_ref, y_ref):
    y_ref[...] = x_ref[...] + jnp.dot(o_ref[...], w_ref[...], preferred_element_type=f32)


def _outproj(x, o, w):
    m = x.shape[0]
    return pl.pallas_call(
        _outproj_body,
        grid=(m // OUT_TM,),
        in_specs=[
            pl.BlockSpec((OUT_TM, D_MODEL), lambda i: (i, 0)),
            pl.BlockSpec((OUT_TM, D_MODEL), lambda i: (i, 0)),
            pl.BlockSpec((D_MODEL, D_MODEL), lambda i: (0, 0)),
        ],
        out_specs=pl.BlockSpec((OUT_TM, D_MODEL), lambda i: (i, 0)),
        out_shape=jax.ShapeDtypeStruct((m, D_MODEL), f32),
        compiler_params=_cparams(("parallel",)),
        name="out_proj",
    )(x, o, w)


AB_TM = 256


def _half_masks(shape):
    lane = lax.broadcasted_iota(i32, shape, 1)
    return lane < 64, (lane % 64) < 32


def _norm_rope_tile(t, g128, cosv, sinv, lo, first):
    ss = t * t
    s_lo = jnp.sum(jnp.where(lo, ss, 0.0), axis=-1, keepdims=True)
    s_hi = jnp.sum(jnp.where(lo, 0.0, ss), axis=-1, keepdims=True)
    r = jnp.where(lo, lax.rsqrt(s_lo / 64.0 + EPS), lax.rsqrt(s_hi / 64.0 + EPS))
    y = (t * r) * g128
    other = jnp.where(first, pltpu.roll(y, 96, 1), pltpu.roll(y, 32, 1))
    return y * cosv + other * sinv


def _abproj_body(x_ref, mg_ref, win_ref, gql_ref, wqu_ref, gkv_ref, gqn_ref, gqr_ref, gkr_ref,
                 gfq_ref, gfk_ref, bf_ref, cos_ref, sin_ref,
                 qmla_ref, ckv_ref, krope_ref, kr128_ref, fq_ref, fk_ref, fv_ref, logf_ref):
    n = _rms(x_ref[...], mg_ref[...]).astype(bf16)
    y = jnp.dot(n, win_ref[...], preferred_element_type=f32)
    cosv, sinv = cos_ref[...], sin_ref[...]
    lo, first = _half_masks(cosv.shape)

    ql = _rms(y[:, 0:Q_LORA], gql_ref[...]).astype(bf16)
    q = jnp.dot(ql, wqu_ref[...], preferred_element_type=f32)
    for h in range(H_A):
        t = _rms(q[:, h * NOPE_D:(h + 1) * NOPE_D], gqn_ref[...]) * MLA_SCALE
        qmla_ref[:, h * QH:h * QH + NOPE_D] = t.astype(bf16)
    base = H_A * NOPE_D
    for p in range(H_A // 2):
        t = _norm_rope_tile(q[:, base + p * LANES:base + (p + 1) * LANES], gqr_ref[...],
                            cosv, sinv, lo, first) * MLA_SCALE
        even = jnp.where(lo, t, 0.0)
        odd = jnp.where(lo, pltpu.roll(t, 64, 1), 0.0)
        qmla_ref[:, (2 * p) * QH + NOPE_D:(2 * p + 1) * QH] = even.astype(bf16)
        qmla_ref[:, (2 * p + 1) * QH + NOPE_D:(2 * p + 2) * QH] = odd.astype(bf16)

    ckv_ref[...] = _rms(y[:, 512:1024], gkv_ref[...])

    for h in range(H_B):
        t = _rms(y[:, 1024 + h * HD_B:1024 + (h + 1) * HD_B], gfq_ref[...]) * FOX_SCALE
        fq_ref[:, h * HD_B:(h + 1) * HD_B] = t.astype(bf16)
    for k in range(KV_B):
        fk_ref[:, k * HD_B:(k + 1) * HD_B] = _rms(y[:, 2048 + k * HD_B:2048 + (k + 1) * HD_B], gfk_ref[...])
    fv_ref[...] = y[:, 2304:2560]

    t = y[:, 2560:2688]
    kr = _norm_rope_tile(t, gkr_ref[...], cosv, sinv, lo, first)
    krope_ref[...] = kr[:, 0:ROPE_D]
    kr128_ref[...] = jnp.where(lo, kr, 0.0).astype(bf16)
    z = t + bf_ref[...]
    ls = jnp.minimum(z, 0.0) - jnp.log(1.0 + jnp.exp(-jnp.abs(z)))
    logf_ref[...] = ls[:, 64:64 + H_B]


def _abproj(x, mg, win, gql, wqu, gkv, gqn, gqr128, gkr128, gfq, gfk, bf128, cos128, sin128):
    m = x.shape[0]
    row = lambda w: pl.BlockSpec((AB_TM, w), lambda i: (i, 0))
    full = lambda a: pl.BlockSpec(a.shape, lambda i: (0,) * a.ndim)
    ins = [x, mg, win, gql, wqu, gkv, gqn, gqr128, gkr128, gfq, gfk, bf128, cos128, sin128]
    in_specs = [row(D_MODEL)] + [full(a) for a in ins[1:12]] + [row(LANES), row(LANES)]
    outs = [(H_A * QH, bf16), (KV_LORA, f32), (ROPE_D, f32), (LANES, bf16), (H_B * HD_B, bf16),
            (KV_B * HD_B, f32), (KV_B * HD_B, f32), (H_B, f32)]
    return pl.pallas_call(
        _abproj_body,
        grid=(m // AB_TM,),
        in_specs=in_specs,
        out_specs=[row(w) for w, _ in outs],
        out_shape=[jax.ShapeDtypeStruct((m, w), dt) for w, dt in outs],
        compiler_params=_cparams(("parallel",)),
        name="ab_proj",
    )(*ins)


KV_TM = 512


def _kvup_body(ckv_ref, kr128_ref, wuk_ref, wuv_ref, gkn_ref, kmla_ref, va_ref):
    c = ckv_ref[...].astype(bf16)
    kk = jnp.dot(c, wuk_ref[...], preferred_element_type=f32)
    kr = kr128_ref[...]
    for h in range(H_A):
        kmla_ref[:, h * QH:h * QH + NOPE_D] = _rms(kk[:, h * NOPE_D:(h + 1) * NOPE_D], gkn_ref[...]).astype(bf16)
        kmla_ref[:, h * QH + NOPE_D:(h + 1) * QH] = kr
    va_ref[...] = jnp.dot(c, wuv_ref[...], preferred_element_type=f32).astype(bf16)


def _kvup(ckv, kr128, wuk, wuv, gkn):
    m = MP
    row = lambda w: pl.BlockSpec((KV_TM, w), lambda i: (i, 0))
    full = lambda a: pl.BlockSpec(a.shape, lambda i: (0,) * a.ndim)
    return pl.pallas_call(
        _kvup_body,
        grid=(m // KV_TM,),
        in_specs=[row(KV_LORA), row(LANES), full(wuk), full(wuv), full(gkn)],
        out_specs=[row(H_A * QH), row(H_A * V_D)],
        out_shape=[jax.ShapeDtypeStruct((m, H_A * QH), bf16), jax.ShapeDtypeStruct((m, H_A * V_D), bf16)],
        compiler_params=_cparams(("parallel",)),
        name="mla_kv_up",
    )(ckv, kr128, wuk, wuv, gkn)


CS_CHUNK = MXU_DIM


def _tri_ones(n):
    r = lax.broadcasted_iota(i32, (n, n), 0)
    c = lax.broadcasted_iota(i32, (n, n), 1)
    return jnp.where(r <= c, 1.0, 0.0).astype(bf16)


def _cumsum_chunks(parts_of, n_chunks, write, carry):
    tri = _tri_ones(CS_CHUNK)
    for ch in range(n_chunks):
        cs = carry
        for p in parts_of(ch):
            cs = cs + jnp.dot(p, tri, preferred_element_type=f32)
        write(ch, cs)
        carry = cs[:, CS_CHUNK - 1:CS_CHUNK]
    return carry


def _cumsum_body(x_ref, o_ref):
    def parts_of(ch):
        return _split3(x_ref[:, ch * CS_CHUNK:(ch + 1) * CS_CHUNK])

    def write(ch, cs):
        o_ref[:, ch * CS_CHUNK:(ch + 1) * CS_CHUNK] = cs

    _cumsum_chunks(parts_of, SEQ // CS_CHUNK, write, jnp.zeros((H_B, 1), f32))


def _cumsum_prompt(lft):
    return pl.pallas_call(
        _cumsum_body,
        grid=(BATCH,),
        in_specs=[pl.BlockSpec((None, H_B, SEQ), lambda b: (b, 0, 0))],
        out_specs=pl.BlockSpec((None, H_B, SEQ), lambda b: (b, 0, 0)),
        out_shape=jax.ShapeDtypeStruct((BATCH, H_B, SEQ), f32),
        compiler_params=_cparams(("parallel",)),
        name="fox_cumsum_prompt",
    )(lft)


FL_T = 512


def _flash_body(*refs, NH, G, DQ, DV, has_bias, nk):
    if has_bias:
        q_ref, k_ref, v_ref, fq_ref, fk_ref, o_ref, m_scr, acc_scr = refs
    else:
        q_ref, k_ref, v_ref, o_ref, m_scr, acc_scr = refs
    assert DV == LANES
    i = pl.program_id(2)
    j = pl.program_id(3)

    @pl.when(j == 0)
    def _():
        m_scr[...] = jnp.full(m_scr.shape, NEG_INF, f32)
        acc_scr[...] = jnp.zeros(acc_scr.shape, f32)

    def tile(diagonal):
        if diagonal:
            mask = (lax.broadcasted_iota(i32, (FL_T, FL_T), 1) <= lax.broadcasted_iota(i32, (FL_T, FL_T), 0))
        ones = jnp.ones((FL_T, LANES), bf16)
        for kv in range(NH):
            k = k_ref[:, kv * DQ:(kv + 1) * DQ].astype(bf16)
            v1 = jnp.concatenate([v_ref[:, kv * DV:(kv + 1) * DV].astype(bf16), ones], axis=1)
            for g in range(G):
                h = kv * G + g
                s = lax.dot_general(q_ref[:, h * DQ:(h + 1) * DQ], k, NT, preferred_element_type=f32)
                if has_bias:
                    s = s + (fq_ref[:, h:h + 1] - fk_ref[h:h + 1, :])
                if diagonal:
                    s = jnp.where(mask, s, NEG_INF)
                m_prev = m_scr[h]
                m_new = jnp.maximum(m_prev, jnp.max(s, axis=-1, keepdims=True))
                alpha = jnp.exp(m_prev - m_new)
                p = jnp.exp(s - jnp.concatenate([m_new] * (FL_T // LANES), axis=1))
                acc_scr[h] = (jnp.concatenate([alpha, alpha], axis=1) * acc_scr[h]
                              + jnp.dot(p.astype(bf16), v1, preferred_element_type=f32))
                m_scr[h] = m_new

    @pl.when(j < i)
    def _():
        tile(False)

    @pl.when(j == i)
    def _():
        tile(True)

    @pl.when(j == nk - 1)
    def _():
        for h in range(NH * G):
            acc = acc_scr[h]
            o_ref[:, h * DV:(h + 1) * DV] = (acc[:, 0:DV] / acc[:, DV:2 * DV]).astype(o_ref.dtype)


def _flash_prompt(q, k, v, n_kv, NH, G, DQ, DV, fq=None, fk=None):
    nq = SEQ // FL_T
    has_bias = fq is not None
    qmap = lambda b, h, i, j: (b * nq + i, h)
    kmap = lambda b, h, i, j: (b * nq + jnp.minimum(i, j), h)
    in_specs = [pl.BlockSpec((FL_T, NH * G * DQ), qmap), pl.BlockSpec((FL_T, NH * DQ), kmap),
                pl.BlockSpec((FL_T, NH * DV), kmap)]
    ins = [q, k, v]
    if has_bias:
        assert NH * G == H_B
        in_specs += [pl.BlockSpec((FL_T, H_B), lambda b, h, i, j: (b * nq + i, 0)),
                     pl.BlockSpec((None, H_B, FL_T), lambda b, h, i, j: (b, 0, jnp.minimum(i, j)))]
        ins += [fq, fk]
    nh = NH * G
    return pl.pallas_call(
        functools.partial(_flash_body, NH=NH, G=G, DQ=DQ, DV=DV, has_bias=has_bias, nk=nq),
        grid=(BATCH, n_kv // NH, nq, nq),
        in_specs=in_specs,
        out_specs=pl.BlockSpec((FL_T, nh * DV), qmap),
        out_shape=jax.ShapeDtypeStruct((MP, n_kv * G * DV), bf16),
        scratch_shapes=[pltpu.VMEM((nh, FL_T, LANES), f32), pltpu.VMEM((nh, FL_T, 2 * DV), f32)],
        compiler_params=_cparams(("parallel", "parallel", "parallel", "arbitrary")),
        name="flash_prompt_bias" if has_bias else "flash_prompt",
    )(*ins)


def _page_copies(pt_ref, layer, b, c, slot, pages, specs):
    out = []
    for g in range(pages):
        page = pt_ref[b, c * pages + g]
        for hbm_ref, buf_ref, sem_ref, rows in specs:
            if rows:
                dst = buf_ref.at[slot, pl.ds(g * rows, rows), :]
            else:
                dst = buf_ref.at[slot, :, pl.ds(g * PAGE_SIZE, PAGE_SIZE)]
            out.append(pltpu.make_async_copy(hbm_ref.at[layer, page], dst, sem_ref.at[slot]))
    return out


def _gather_wait(pt_ref, layer, n_chunks, pages, specs):
    b, c = pl.program_id(0), pl.program_id(1)
    n = b * n_chunks + c
    slot = n % 2

    @pl.when(n == 0)
    def _():
        for cp in _page_copies(pt_ref, layer, 0, 0, 0, pages, specs):
            cp.start()

    for cp in _page_copies(pt_ref, layer, b, c, slot, pages, specs):
        cp.wait()
    return slot


def _gather_prefetch(pt_ref, layer, n_chunks, pages, specs):
    n = pl.program_id(0) * n_chunks + pl.program_id(1)
    nxt = (n + 1) % (pl.num_programs(0) * n_chunks)
    for cp in _page_copies(pt_ref, layer, nxt // n_chunks, nxt % n_chunks, (n + 1) % 2, pages, specs):
        cp.start()


def _gather_drain(pt_ref, layer, n_chunks, pages, specs):
    n = pl.program_id(0) * n_chunks + pl.program_id(1)

    @pl.when(n == pl.num_programs(0) * n_chunks - 1)
    def _():
        for cp in _page_copies(pt_ref, layer, 0, 0, (n + 1) % 2, pages, specs):
            cp.wait()


def _softmax_update(s, v, m_ref, l_ref, acc_ref):
    m_prev = m_ref[...]
    m_new = jnp.maximum(m_prev, jnp.max(s, axis=-1, keepdims=True))
    alpha = jnp.exp(m_prev - m_new)
    p = jnp.exp(s - m_new)
    l_ref[...] = alpha * l_ref[...] + jnp.sum(p, axis=-1, keepdims=True)
    acc_ref[...] = alpha * acc_ref[...] + jnp.dot(p.astype(bf16), v, preferred_element_type=f32)
    m_ref[...] = m_new


def _new_key_mask(rows):
    t = lax.broadcasted_iota(i32, (rows, LANES), 0) % DEC_SEQ
    col = lax.broadcasted_iota(i32, (rows, LANES), 1)
    return (col < DEC_SEQ) & (col <= t)


LF_ROWS = PAST_LEN + CS_CHUNK
LF_SB = 16


def _lf_copies(pt_ref, cache_ref, buf, sem, layer, step, slot, start):
    def one_seq(sl, carry):
        for p in range(N_PAGES):
            cp = pltpu.make_async_copy(cache_ref.at[layer, pt_ref[step * LF_SB + sl, p]],
                                       buf.at[slot, pl.ds(sl * H_B, H_B), pl.ds(p * PAGE_SIZE, PAGE_SIZE)],
                                       sem.at[slot])
            if start:
                cp.start()
            else:
                cp.wait()
        return carry
    lax.fori_loop(0, LF_SB, one_seq, 0)


def _lf_sample_body(pt_ref, cache_ref, lfn_ref, fk_ref, fn_ref, buf, sem, *, layer):
    n = pl.program_id(0)
    slot = n % 2

    @pl.when(n == 0)
    def _():
        _lf_copies(pt_ref, cache_ref, buf, sem, layer, 0, 0, True)

    @pl.when(n + 1 < pl.num_programs(0))
    def _():
        _lf_copies(pt_ref, cache_ref, buf, sem, layer, n + 1, (n + 1) % 2, True)

    _lf_copies(pt_ref, cache_ref, buf, sem, layer, n, slot, False)
    buf[slot, :, PAST_LEN:LF_ROWS] = lfn_ref[...]

    def parts_of(ch):
        return _split3(buf[slot, :, ch * CS_CHUNK:(ch + 1) * CS_CHUNK])

    def write(ch, cs):
        if ch < PAST_LEN // CS_CHUNK:
            fk_ref[:, ch * CS_CHUNK:(ch + 1) * CS_CHUNK] = cs
        else:
            fn_ref[...] = cs[:, 0:LANES]

    _cumsum_chunks(parts_of, LF_ROWS // CS_CHUNK, write, jnp.zeros((LF_SB * H_B, 1), f32))


def _lf_sample(page_table, cache_logf_t, lf_new_t, layer):
    rows = LF_SB * H_B
    grid_spec = pltpu.PrefetchScalarGridSpec(
        num_scalar_prefetch=1,
        grid=(DEC_BATCH // LF_SB,),
        in_specs=[pl.BlockSpec(memory_space=pl.ANY),
                  pl.BlockSpec((rows, CS_CHUNK), lambda n, pt: (n, 0))],
        out_specs=[pl.BlockSpec((rows, PAST_LEN), lambda n, pt: (n, 0)),
                   pl.BlockSpec((rows, LANES), lambda n, pt: (n, 0))],
        scratch_shapes=[pltpu.VMEM((2, rows, LF_ROWS), f32), pltpu.SemaphoreType.DMA((2,))],
    )
    return pl.pallas_call(
        functools.partial(_lf_sample_body, layer=layer),
        grid_spec=grid_spec,
        out_shape=[jax.ShapeDtypeStruct((DEC_BATCH * H_B, PAST_LEN), f32),
                   jax.ShapeDtypeStruct((DEC_BATCH * H_B, LANES), f32)],
        compiler_params=_cparams(("arbitrary",)),
        name="fox_logf_sample",
    )(page_table, cache_logf_t, lf_new_t)


MLA_PAGES = 16
MLA_KEYS = MLA_PAGES * PAGE_SIZE
MLA_CHUNKS = N_PAGES // MLA_PAGES
DEC_SUB = MXU_DIM
QROWS = H_A * DEC_SEQ


def _mla_scores(a_ref, c_bf, rope):
    r = lax.dot_general(a_ref[...], c_bf, NT, preferred_element_type=f32)
    out = []
    for h in range(H_A):
        kk = r[h * NOPE_D:(h + 1) * NOPE_D, :]
        inv = lax.rsqrt(jnp.sum(kk * kk, axis=0, keepdims=True) / NOPE_D + EPS)
        lo_, hi_ = H_A * NOPE_D + h * DEC_SEQ, H_A * NOPE_D + (h + 1) * DEC_SEQ
        out.append(r[lo_:hi_, :] * inv)
    return jnp.concatenate(out, axis=0) + rope


def _mla_decode_body(pt_ref, ckv_hbm, kr_hbm, q_ref, cn_ref, krn_ref, wukt_ref, wuv_ref, gkn_ref, o_ref,
                     cbuf, krbuf, csem, krsem, a_scr, qr_scr, s_scr, cbf_scr, cn_pad, krn_pad, m_scr, l_scr, acc_scr,
                     *, layer):
    b, c = pl.program_id(0), pl.program_id(1)
    nk = H_A * NOPE_D

    @pl.when((b == 0) & (c == 0))
    def _():
        a_scr[0:nk, :] = wukt_ref[...]
        cn_pad[...] = jnp.zeros(cn_pad.shape, f32)
        krn_pad[...] = jnp.zeros(krn_pad.shape, f32)

    gather = (pt_ref, layer, MLA_CHUNKS, MLA_PAGES, [(ckv_hbm, cbuf, csem, PAGE_SIZE), (kr_hbm, krbuf, krsem, 0)])
    slot = _gather_wait(*gather)

    @pl.when(c == 0)
    def _():
        q = q_ref[...]
        wq, qr_rows = [], []
        for h in range(H_A):
            qg = (q[:, h * QH:h * QH + NOPE_D] * gkn_ref[...]).astype(bf16)
            wq.append(jnp.dot(qg, wukt_ref[h * NOPE_D:(h + 1) * NOPE_D, :], preferred_element_type=f32))
            qr_rows.append(q[:, h * QH + NOPE_D:h * QH + NOPE_D + ROPE_D])
        a_scr[nk:nk + QROWS, :] = jnp.concatenate(wq, axis=0).astype(bf16)
        qr_scr[...] = jnp.concatenate(qr_rows, axis=0).astype(bf16)
        m_scr[...] = jnp.full(m_scr.shape, NEG_INF, f32)
        l_scr[...] = jnp.zeros(l_scr.shape, f32)
        acc_scr[...] = jnp.zeros(acc_scr.shape, f32)

    _gather_prefetch(*gather)
    qr = qr_scr[...]
    for sub in range(MLA_KEYS // DEC_SUB):
        keys = pl.ds(sub * DEC_SUB, DEC_SUB)
        rope = jnp.dot(qr, krbuf[slot, :, keys].astype(bf16), preferred_element_type=f32)
        c_bf = cbuf[slot, keys, :].astype(bf16)
        cbf_scr[sub * DEC_SUB:(sub + 1) * DEC_SUB, :] = c_bf
        s_scr[:, sub * DEC_SUB:(sub + 1) * DEC_SUB] = _mla_scores(a_scr, c_bf, rope)
    _softmax_update(s_scr[...], cbf_scr[...], m_scr, l_scr, acc_scr)

    @pl.when(c == MLA_CHUNKS - 1)
    def _():
        cn_pad[0:DEC_SEQ, :] = cn_ref[...]
        krn_pad[0:DEC_SEQ, :] = krn_ref[...]
        cn = cn_pad[...].astype(bf16)
        rope = lax.dot_general(qr, krn_pad[...].astype(bf16), NT, preferred_element_type=f32)
        s = _mla_scores(a_scr, cn, rope)
        s = jnp.where(_new_key_mask(QROWS), s, NEG_INF)
        _softmax_update(s, cn, m_scr, l_scr, acc_scr)
        o_lat = (acc_scr[...] / l_scr[...]).astype(bf16)
        for h in range(H_A):
            o_ref[:, h * V_D:(h + 1) * V_D] = jnp.dot(
                o_lat[h * DEC_SEQ:(h + 1) * DEC_SEQ, :], wuv_ref[:, h * V_D:(h + 1) * V_D],
                preferred_element_type=f32).astype(o_ref.dtype)

    _gather_drain(*gather)


def _mla_decode(page_table, cache_ckv, cache_krope, q_s, c_new, kr_new, wukt, wuv, gkn, layer):
    seq = lambda w: pl.BlockSpec((None, DEC_SEQ, w), lambda b, c, pt: (b, 0, 0))
    full = lambda a: pl.BlockSpec(a.shape, lambda b, c, pt: (0,) * a.ndim)
    grid_spec = pltpu.PrefetchScalarGridSpec(
        num_scalar_prefetch=1,
        grid=(DEC_BATCH, MLA_CHUNKS),
        in_specs=[pl.BlockSpec(memory_space=pl.ANY), pl.BlockSpec(memory_space=pl.ANY),
                  seq(H_A * QH), seq(KV_LORA), seq(ROPE_D), full(wukt), full(wuv), full(gkn)],
        out_specs=seq(H_A * V_D),
        scratch_shapes=[
            pltpu.VMEM((2, MLA_KEYS, KV_LORA), f32), pltpu.VMEM((2, ROPE_D, MLA_KEYS), f32),
            pltpu.SemaphoreType.DMA((2,)), pltpu.SemaphoreType.DMA((2,)),
            pltpu.VMEM((H_A * NOPE_D + QROWS, KV_LORA), bf16), pltpu.VMEM((QROWS, ROPE_D), bf16),
            pltpu.VMEM((QROWS, MLA_KEYS), f32), pltpu.VMEM((MLA_KEYS, KV_LORA), bf16),
            pltpu.VMEM((LANES, KV_LORA), f32), pltpu.VMEM((LANES, ROPE_D), f32),
            pltpu.VMEM((QROWS, 1), f32), pltpu.VMEM((QROWS, 1), f32), pltpu.VMEM((QROWS, KV_LORA), f32),
        ],
    )
    return pl.pallas_call(
        functools.partial(_mla_decode_body, layer=layer),
        grid_spec=grid_spec,
        out_shape=jax.ShapeDtypeStruct((DEC_BATCH, DEC_SEQ, H_A * V_D), f32),
        compiler_params=_cparams(("arbitrary", "arbitrary")),
        name="mla_decode",
    )(page_table, cache_ckv, cache_krope, q_s, c_new, kr_new, wukt, wuv, gkn)


GROWS = G_B * DEC_SEQ
FOX_PAGES = 32
FOX_KEYS = FOX_PAGES * PAGE_SIZE
FOX_CHUNKS = N_PAGES // FOX_PAGES


def _fox_decode_body(pt_ref, k_hbm, v_hbm, q_ref, kn_ref, vn_ref, fq_ref, fk_ref, fn_ref, o_ref,
                     kbuf, vbuf, ksem, vsem, kn_pad, vn_pad, q_scr, m_scr, l_scr, acc_scr, *, layer):
    b, c = pl.program_id(0), pl.program_id(1)

    @pl.when((b == 0) & (c == 0))
    def _():
        kn_pad[...] = jnp.zeros(kn_pad.shape, f32)
        vn_pad[...] = jnp.zeros(vn_pad.shape, f32)

    gather = (pt_ref, layer, FOX_CHUNKS, FOX_PAGES,
              [(k_hbm, kbuf, ksem, PAGE_SIZE * KV_B), (v_hbm, vbuf, vsem, PAGE_SIZE * KV_B)])
    slot = _gather_wait(*gather)

    @pl.when(c == 0)
    def _():
        q = q_ref[...]
        q_scr[...] = jnp.concatenate([q[:, h * HD_B:(h + 1) * HD_B] for h in range(H_B)], axis=0).astype(bf16)
        m_scr[...] = jnp.full(m_scr.shape, NEG_INF, f32)
        l_scr[...] = jnp.zeros(l_scr.shape, f32)
        acc_scr[...] = jnp.zeros(acc_scr.shape, f32)

    def bias(fk):
        rows = [fq_ref[h * DEC_SEQ:(h + 1) * DEC_SEQ, :] - fk[h:h + 1, :] for h in range(H_B)]
        return jnp.concatenate(rows, axis=0)

    def attend(kv, k_bf, v_bf, bias_kv, mask):
        rows = slice(kv * GROWS, (kv + 1) * GROWS)
        s = lax.dot_general(q_scr[rows, :], k_bf, NT, preferred_element_type=f32) + bias_kv
        if mask is not None:
            s = jnp.where(mask, s, NEG_INF)
        _softmax_update(s, v_bf, m_scr.at[rows], l_scr.at[rows], acc_scr.at[rows])

    _gather_prefetch(*gather)
    bias_past = bias(fk_ref[...])
    for kv in range(KV_B):
        sel = pl.ds(kv, FOX_KEYS, stride=KV_B)
        attend(kv, kbuf[slot, sel, :].astype(bf16), vbuf[slot, sel, :].astype(bf16),
               bias_past[kv * GROWS:(kv + 1) * GROWS], None)

    @pl.when(c == FOX_CHUNKS - 1)
    def _():
        bias_new = bias(fn_ref[...])
        mask = _new_key_mask(GROWS)
        for kv in range(KV_B):
            kn_pad[0:DEC_SEQ, :] = kn_ref[:, kv * HD_B:(kv + 1) * HD_B]
            vn_pad[0:DEC_SEQ, :] = vn_ref[:, kv * HD_B:(kv + 1) * HD_B]
            attend(kv, kn_pad[...].astype(bf16), vn_pad[...].astype(bf16),
                   bias_new[kv * GROWS:(kv + 1) * GROWS], mask)
        o = acc_scr[...] / l_scr[...]
        for h in range(H_B):
            o_ref[:, h * HD_B:(h + 1) * HD_B] = o[h * DEC_SEQ:(h + 1) * DEC_SEQ, :].astype(o_ref.dtype)

    _gather_drain(*gather)


def _fox_decode(page_table, cache_k, cache_v, q_s, k_new, v_new, fq_col, fk_past, fn_new, layer):
    seq = lambda w: pl.BlockSpec((None, DEC_SEQ, w), lambda b, c, pt: (b, 0, 0))
    grid_spec = pltpu.PrefetchScalarGridSpec(
        num_scalar_prefetch=1,
        grid=(DEC_BATCH, FOX_CHUNKS),
        in_specs=[pl.BlockSpec(memory_space=pl.ANY), pl.BlockSpec(memory_space=pl.ANY),
                  seq(H_B * HD_B), seq(KV_B * HD_B), seq(KV_B * HD_B),
                  pl.BlockSpec((None, H_B * DEC_SEQ, 1), lambda b, c, pt: (b, 0, 0)),
                  pl.BlockSpec((None, H_B, FOX_KEYS), lambda b, c, pt: (b, 0, c)),
                  pl.BlockSpec((None, H_B, LANES), lambda b, c, pt: (b, 0, 0))],
        out_specs=seq(H_B * HD_B),
        scratch_shapes=[
            pltpu.VMEM((2, FOX_KEYS * KV_B, HD_B), f32), pltpu.VMEM((2, FOX_KEYS * KV_B, HD_B), f32),
            pltpu.SemaphoreType.DMA((2,)), pltpu.SemaphoreType.DMA((2,)),
            pltpu.VMEM((LANES, HD_B), f32), pltpu.VMEM((LANES, HD_B), f32),
            pltpu.VMEM((H_B * DEC_SEQ, HD_B), bf16),
            pltpu.VMEM((H_B * DEC_SEQ, 1), f32), pltpu.VMEM((H_B * DEC_SEQ, 1), f32),
            pltpu.VMEM((H_B * DEC_SEQ, HD_B), f32),
        ],
    )
    return pl.pallas_call(
        functools.partial(_fox_decode_body, layer=layer),
        grid_spec=grid_spec,
        out_shape=jax.ShapeDtypeStruct((DEC_BATCH, DEC_SEQ, H_B * HD_B), f32),
        compiler_params=_cparams(("arbitrary", "arbitrary")),
        name="fox_decode",
    )(page_table, cache_k, cache_v, q_s, k_new, v_new, fq_col, fk_past, fn_new)


C_TM = 256
C_IN = (H_C + 2 * KV_C) * HD_C


def _norm_pair_tile(t, g128, lo):
    ss = t * t
    s_lo = jnp.sum(jnp.where(lo, ss, 0.0), axis=-1, keepdims=True)
    s_hi = jnp.sum(jnp.where(lo, 0.0, ss), axis=-1, keepdims=True)
    r = jnp.where(lo, lax.rsqrt(s_lo / 64.0 + EPS), lax.rsqrt(s_hi / 64.0 + EPS))
    return (t * r) * g128


def _cproj_body(x_ref, mg_ref, win_ref, gq_ref, gk_ref, q_ref, k_ref, v_ref):
    n = _rms(x_ref[...], mg_ref[...]).astype(bf16)
    y = jnp.dot(n, win_ref[...], preferred_element_type=f32)
    lo, _ = _half_masks((C_TM, LANES))
    nq = H_C * HD_C
    for p in range(nq // LANES):
        q_ref[:, p * LANES:(p + 1) * LANES] = (
            _norm_pair_tile(y[:, p * LANES:(p + 1) * LANES], gq_ref[...], lo) * C_SCALE).astype(bf16)
    for p in range(KV_C * HD_C // LANES):
        k_ref[:, p * LANES:(p + 1) * LANES] = _norm_pair_tile(y[:, nq + p * LANES:nq + (p + 1) * LANES], gk_ref[...], lo)
    v_ref[...] = y[:, nq + KV_C * HD_C:]


def _cproj(x, mg, win, gq128, gk128):
    m = x.shape[0]
    row = lambda w: pl.BlockSpec((C_TM, w), lambda i: (i, 0))
    full = lambda a: pl.BlockSpec(a.shape, lambda i: (0,) * a.ndim)
    return pl.pallas_call(
        _cproj_body,
        grid=(m // C_TM,),
        in_specs=[row(D_MODEL), full(mg), full(win), full(gq128), full(gk128)],
        out_specs=[row(H_C * HD_C), row(KV_C * HD_C), row(KV_C * HD_C)],
        out_shape=[jax.ShapeDtypeStruct((m, H_C * HD_C), bf16), jax.ShapeDtypeStruct((m, KV_C * HD_C), f32),
                   jax.ShapeDtypeStruct((m, KV_C * HD_C), f32)],
        compiler_params=_cparams(("parallel",)),
        name="swa_proj",
    )(x, mg, win, gq128, gk128)


def _t5_bucket_np(dist):
    n = np.maximum(dist, 0)
    nf = np.maximum(n, 1).astype(np.float32)
    large = MAX_EXACT + (np.log(nf / np.float32(MAX_EXACT)) / np.float32(math.log(MAX_DISTANCE / MAX_EXACT))
                         * np.float32(NUM_BUCKETS - MAX_EXACT)).astype(np.int32)
    return np.where(n < MAX_EXACT, n, np.minimum(large, NUM_BUCKETS - 1)).astype(np.int32)


def _fill_bias(rel_ref, bucket, bias_scr):
    def one_head(h, carry):
        acc = jnp.zeros(bucket.shape, f32)
        for bk in range(NUM_BUCKETS):
            acc = jnp.where(bucket == bk, rel_ref[bk, h], acc)
        bias_scr[h] = acc
        return carry
    lax.fori_loop(0, H_C, one_head, 0)


def _sink_softmax_pv(s, sink, v_bf):
    m = jnp.maximum(jnp.max(s, axis=-1, keepdims=True), sink)
    e = jnp.exp(s - m)
    den = jnp.sum(e, axis=-1, keepdims=True) + jnp.exp(sink - m)
    return jnp.dot((e / den).astype(bf16), v_bf, preferred_element_type=f32)


def _swa_prompt_body(rel_ref, sink_ref, bucket_ref, q_ref, kp_ref, kc_ref, vp_ref, vc_ref, o_ref, bias_scr):
    b, n = pl.program_id(0), pl.program_id(1)

    @pl.when((b == 0) & (n == 0))
    def _():
        _fill_bias(rel_ref, bucket_ref[...], bias_scr)

    kb = jnp.concatenate([kp_ref[...], kc_ref[...]], axis=0).astype(bf16)
    vb = jnp.concatenate([vp_ref[...], vc_ref[...]], axis=0).astype(bf16)
    row = lax.broadcasted_iota(i32, (WINDOW, 2 * WINDOW), 0) + WINDOW
    col = lax.broadcasted_iota(i32, (WINDOW, 2 * WINDOW), 1)
    dist = row - col
    mask = (dist >= 0) & (dist < WINDOW) & ((n > 0) | (col >= WINDOW))
    for kv in range(KV_C):
        k_h = kb[:, kv * HD_C:(kv + 1) * HD_C]
        v_h = vb[:, kv * HD_C:(kv + 1) * HD_C]
        outs = []
        for g in range(G_C):
            h = kv * G_C + g
            s = lax.dot_general(q_ref[:, h * HD_C:(h + 1) * HD_C], k_h, NT, preferred_element_type=f32) + bias_scr[h]
            s = jnp.where(mask, s, NEG_INF)
            outs.append(_sink_softmax_pv(s, sink_ref[h], v_h))
        for p in range(G_C // 2):
            tile = jnp.concatenate([outs[2 * p], outs[2 * p + 1]], axis=1)
            col0 = (kv * G_C + 2 * p) * HD_C
            o_ref[:, col0:col0 + LANES] = tile.astype(o_ref.dtype)


def _swa_prompt(q, k, v, rel_bias, sinks):
    nb = SEQ // WINDOW
    bucket = jnp.asarray(_t5_bucket_np(WINDOW + np.arange(WINDOW)[:, None] - np.arange(2 * WINDOW)[None, :]))
    smem = pl.BlockSpec(memory_space=pltpu.SMEM)
    cur = lambda w: pl.BlockSpec((WINDOW, w), lambda b, n: (b * nb + n, 0))
    prev = lambda w: pl.BlockSpec((WINDOW, w), lambda b, n: (b * nb + jnp.maximum(n - 1, 0), 0))
    kvw = KV_C * HD_C
    return pl.pallas_call(
        _swa_prompt_body,
        grid=(BATCH, nb),
        in_specs=[smem, smem, pl.BlockSpec(bucket.shape, lambda b, n: (0, 0)),
                  cur(H_C * HD_C), prev(kvw), cur(kvw), prev(kvw), cur(kvw)],
        out_specs=cur(H_C * HD_C),
        out_shape=jax.ShapeDtypeStruct((MP, H_C * HD_C), bf16),
        scratch_shapes=[pltpu.VMEM((H_C, WINDOW, 2 * WINDOW), f32)],
        compiler_params=_cparams(("arbitrary", "arbitrary")),
        name="swa_prompt",
    )(rel_bias, sinks, bucket, q, k, k, v, v)


SWA_SB = 4


def _swa_decode_body(rel_ref, sink_ref, bucket_ref, q_ref, kb_ref, vb_ref, kn_ref, vn_ref,
                     o_ref, ko_ref, vo_ref, bias_scr, kn_pad, vn_pad):
    b = pl.program_id(0)

    @pl.when(b == 0)
    def _():
        _fill_bias(rel_ref, bucket_ref[...], bias_scr)
        kn_pad[...] = jnp.zeros(kn_pad.shape, f32)
        vn_pad[...] = jnp.zeros(vn_pad.shape, f32)

    t = lax.broadcasted_iota(i32, (DEC_SEQ, 2 * WINDOW), 0)
    col = lax.broadcasted_iota(i32, (DEC_SEQ, 2 * WINDOW), 1)
    dist = WINDOW + t - col
    mask = (dist >= 0) & (dist < WINDOW)
    keep = WINDOW - DEC_SEQ
    for sl in range(SWA_SB):
        kn_pad[sl, 0:DEC_SEQ, :] = kn_ref[sl]
        vn_pad[sl, 0:DEC_SEQ, :] = vn_ref[sl]
        keys = jnp.concatenate([kb_ref[sl], kn_pad[sl]], axis=0).astype(bf16)
        vals = jnp.concatenate([vb_ref[sl], vn_pad[sl]], axis=0).astype(bf16)
        q = q_ref[sl]
        for kv in range(KV_C):
            k_h = keys[:, kv * HD_C:(kv + 1) * HD_C]
            v_h = vals[:, kv * HD_C:(kv + 1) * HD_C]
            q_kv = jnp.concatenate([q[:, (kv * G_C + g) * HD_C:(kv * G_C + g + 1) * HD_C] for g in range(G_C)],
                                   axis=0).astype(bf16)
            s_all = lax.dot_general(q_kv, k_h, NT, preferred_element_type=f32)
            outs = []
            for g in range(G_C):
                h = kv * G_C + g
                s = s_all[g * DEC_SEQ:(g + 1) * DEC_SEQ, :] + bias_scr[h]
                s = jnp.where(mask, s, NEG_INF)
                outs.append(_sink_softmax_pv(s, sink_ref[h], v_h))
            for p in range(G_C // 2):
                tile = jnp.concatenate([outs[2 * p], outs[2 * p + 1]], axis=1)
                col0 = (kv * G_C + 2 * p) * HD_C
                o_ref[sl, :, col0:col0 + LANES] = tile.astype(o_ref.dtype)
        ko_ref[sl, 0:keep, :] = kb_ref[sl, DEC_SEQ:WINDOW, :]
        ko_ref[sl, keep:WINDOW, :] = kn_ref[sl]
        vo_ref[sl, 0:keep, :] = vb_ref[sl, DEC_SEQ:WINDOW, :]
        vo_ref[sl, keep:WINDOW, :] = vn_ref[sl]


def _swa_decode(q_s, buf_k, buf_v, k_new, v_new, rel_bias, sinks):
    bucket = jnp.asarray(_t5_bucket_np(WINDOW + np.arange(DEC_SEQ)[:, None] - np.arange(2 * WINDOW)[None, :]))
    smem = pl.BlockSpec(memory_space=pltpu.SMEM)
    kvw = KV_C * HD_C
    seq = lambda r, w: pl.BlockSpec((SWA_SB, r, w), lambda b: (b, 0, 0))
    return pl.pallas_call(
        _swa_decode_body,
        grid=(DEC_BATCH // SWA_SB,),
        in_specs=[smem, smem, pl.BlockSpec(bucket.shape, lambda b: (0, 0)),
                  seq(DEC_SEQ, H_C * HD_C), seq(WINDOW, kvw), seq(WINDOW, kvw), seq(DEC_SEQ, kvw), seq(DEC_SEQ, kvw)],
        out_specs=[seq(DEC_SEQ, H_C * HD_C), seq(WINDOW, kvw), seq(WINDOW, kvw)],
        out_shape=[jax.ShapeDtypeStruct((DEC_BATCH, DEC_SEQ, H_C * HD_C), f32),
                   jax.ShapeDtypeStruct((DEC_BATCH, WINDOW, kvw), f32),
                   jax.ShapeDtypeStruct((DEC_BATCH, WINDOW, kvw), f32)],
        scratch_shapes=[pltpu.VMEM((H_C, DEC_SEQ, 2 * WINDOW), f32), pltpu.VMEM((SWA_SB, WINDOW, kvw), f32),
                        pltpu.VMEM((SWA_SB, WINDOW, kvw), f32)],
        compiler_params=_cparams(("arbitrary",)),
        name="swa_decode",
    )(rel_bias, sinks, bucket, q_s, buf_k, buf_v, k_new, v_new)


def _rope_tables():
    half = ROPE_D // 2
    inv = ROPE_THETA ** (-jnp.arange(half, dtype=f32) / half)
    pos = jnp.concatenate([jnp.tile(jnp.arange(SEQ, dtype=i32), BATCH),
                           jnp.tile(PAST_LEN + jnp.arange(DEC_SEQ, dtype=i32), DEC_BATCH)])
    ang = pos.astype(f32)[:, None] * inv[None, :]
    cos, sin = jnp.cos(ang), jnp.sin(ang)
    return jnp.tile(cos, (1, 4)), jnp.tile(jnp.concatenate([-sin, sin], axis=1), (1, 2))


def _row(v, reps=1):
    return jnp.tile(v.astype(f32), reps).reshape(1, -1)


def _ab_layer(x, j, tabs, caches, page_table, p):
    cos128, sin128 = tabs
    cache_ckv, cache_krope, cache_k, cache_v, cache_logf = caches
    w_in = p["ab_w_in"][j]
    pad = jnp.zeros((D_MODEL, IN_AB_PAD - w_in.shape[1]), f32)
    win = jnp.concatenate([w_in[:, 0:1024], w_in[:, 1088:2624], w_in[:, 1024:1088], w_in[:, 2624:2632], pad],
                          axis=1).astype(bf16)
    wq = p["mla_w_q_up"][j].reshape(Q_LORA, H_A, NOPE_D + ROPE_D)
    wqu = jnp.concatenate([wq[:, :, :NOPE_D].reshape(Q_LORA, -1), wq[:, :, NOPE_D:].reshape(Q_LORA, -1)],
                          axis=1).astype(bf16)
    bf128 = jnp.zeros((1, LANES), f32).at[0, 64:64 + H_B].set(p["fox_b_f"][j])
    gkr128 = jnp.concatenate([p["mla_g_kr"][j], jnp.ones((64,), f32)]).reshape(1, LANES)
    qmla, ckv, krope, kr128, fq, fk, fv, logf = _abproj(
        x, _row(p["mix_g"][2 * j]), win, _row(p["mla_g_q_lat"][j]), wqu, _row(p["mla_g_kv_lat"][j]),
        _row(p["mla_g_qn"][j]), _row(p["mla_g_qr"][j], 2), gkr128, _row(p["fox_g_q"][j]), _row(p["fox_g_k"][j]),
        bf128, cos128, sin128)
    wuk = p["mla_w_uk"][j].astype(bf16)
    wuv = p["mla_w_uv"][j].astype(bf16)
    gkn = _row(p["mla_g_kn"][j])
    proj = (qmla, ckv, krope, kr128, fq, fk, fv, logf)
    o_p = _ab_prompt_attn(proj, wuk, wuv, gkn)
    o_s = _ab_sample_attn(proj, caches, page_table, wuk, wuv, gkn, j)
    x = _outproj(x, jnp.concatenate([o_p, o_s], axis=0), p["ab_w_out"][j].astype(bf16))
    seqs = lambda a: a[MP:].reshape(DEC_BATCH, DEC_SEQ, -1)
    st_p = (ckv[:MP].reshape(BATCH, SEQ, KV_LORA), krope[:MP].reshape(BATCH, SEQ, ROPE_D),
            fk[:MP].reshape(BATCH, SEQ, KV_B, HD_B), fv[:MP].reshape(BATCH, SEQ, KV_B, HD_B),
            logf[:MP].reshape(BATCH, SEQ, H_B))
    st_s = (seqs(ckv), seqs(krope), seqs(fk).reshape(DEC_BATCH, DEC_SEQ, KV_B, HD_B),
            seqs(fv).reshape(DEC_BATCH, DEC_SEQ, KV_B, HD_B), seqs(logf))
    return x, st_p, st_s


def _ab_prompt_attn(proj, wuk, wuv, gkn):
    qmla, ckv, krope, kr128, fq, fk, fv, logf = proj
    kmla, va = _kvup(ckv, kr128, wuk, wuv, gkn)
    o_a = _flash_prompt(qmla, kmla, va, H_A, 2, 1, QH, V_D)
    lf_p = logf[:MP].reshape(BATCH, SEQ, H_B)
    f_t = _cumsum_prompt(jnp.swapaxes(lf_p, 1, 2))
    f_tok = jnp.swapaxes(f_t, 1, 2).reshape(MP, H_B)
    o_b = _flash_prompt(fq, fk, fv, KV_B, KV_B, G_B, HD_B, HD_B, fq=f_tok, fk=f_t)
    return jnp.concatenate([o_a, o_b], axis=1)


def _ab_sample_attn(proj, caches, page_table, wuk, wuv, gkn, j):
    qmla, ckv, krope, kr128, fq, fk, fv, logf = proj
    cache_ckv, cache_krope_t, cache_k, cache_v, cache_logf_t = caches
    seqs = lambda a: a[MP:].reshape(DEC_BATCH, DEC_SEQ, -1)
    lf_new_t = jnp.swapaxes(seqs(logf), 1, 2).reshape(DEC_BATCH * H_B, DEC_SEQ)
    lf_new_t = jnp.pad(lf_new_t, ((0, 0), (0, CS_CHUNK - DEC_SEQ)))
    fk_past, fn_new = _lf_sample(page_table, cache_logf_t, lf_new_t, j)
    fk_past = fk_past.reshape(DEC_BATCH, H_B, PAST_LEN)
    fn_new = fn_new.reshape(DEC_BATCH, H_B, LANES)
    fq_col = fn_new[:, :, :DEC_SEQ].reshape(DEC_BATCH, H_B * DEC_SEQ, 1)
    os_a = _mla_decode(page_table, cache_ckv, cache_krope_t, seqs(qmla).astype(f32), seqs(ckv), seqs(krope),
                       wuk.T, wuv, gkn, j)
    os_b = _fox_decode(page_table, cache_k, cache_v, seqs(fq).astype(f32), seqs(fk), seqs(fv),
                       fq_col, fk_past, fn_new, j)
    return jnp.concatenate([os_a, os_b], axis=2).reshape(MS, D_MODEL).astype(bf16)


def _c_layer(x, j, state_k, state_v, p):
    kvw = KV_C * HD_C
    q, k, v = _cproj(x, _row(p["mix_g"][2 * j + 1]), p["swa_w_in"][j].astype(bf16),
                     _row(p["swa_g_q"][j], 2), _row(p["swa_g_k"][j], 2))
    rel = p["rel_bias"].astype(f32)
    sinks = p["swa_sinks"][j].astype(f32)
    o_p = _swa_prompt(q, k, v, rel, sinks)
    seqs = lambda a: a[MP:].reshape(DEC_BATCH, DEC_SEQ, -1)
    o_s, k_out, v_out = _swa_decode(seqs(q).astype(f32), state_k[j].reshape(DEC_BATCH, WINDOW, kvw),
                                    state_v[j].reshape(DEC_BATCH, WINDOW, kvw), seqs(k), seqs(v), rel, sinks)
    o = jnp.concatenate([o_p, o_s.reshape(MS, D_MODEL).astype(bf16)], axis=0)
    x = _outproj(x, o, p["swa_w_out"][j].astype(bf16))
    tail = lambda a: a[:MP].reshape(BATCH, SEQ, KV_C, HD_C)[:, SEQ - WINDOW:]
    st_p = (tail(k), tail(v))
    st_s = (k_out.reshape(DEC_BATCH, WINDOW, KV_C, HD_C), v_out.reshape(DEC_BATCH, WINDOW, KV_C, HD_C))
    return x, st_p, st_s


def kernel(x_prompt, x_sample, cache_mla_ckv, cache_mla_krope, cache_fox_k, cache_fox_v, cache_fox_logf, state_swa_k, state_swa_v, page_table, ffn1_g, ffn1_w_gate, ffn1_w_up, ffn1_w_down, mix_g, ffn2_g, ffn2_w_gate, ffn2_w_up, ffn2_w_down, ab_w_in, mla_g_q_lat, mla_w_q_up, mla_g_kv_lat, mla_w_uk, mla_w_uv, mla_g_qn, mla_g_qr, mla_g_kn, mla_g_kr, fox_g_q, fox_g_k, fox_b_f, ab_w_out, swa_w_in, swa_g_q, swa_g_k, swa_sinks, swa_w_out, rel_bias):
    assert WINDOW == PAGE_SIZE == LANES and min(WINDOW, PAST_LEN) == WINDOW
    p = dict(mix_g=mix_g, ab_w_in=ab_w_in, mla_g_q_lat=mla_g_q_lat, mla_w_q_up=mla_w_q_up,
             mla_g_kv_lat=mla_g_kv_lat, mla_w_uk=mla_w_uk, mla_w_uv=mla_w_uv, mla_g_qn=mla_g_qn,
             mla_g_qr=mla_g_qr, mla_g_kn=mla_g_kn, mla_g_kr=mla_g_kr, fox_g_q=fox_g_q, fox_g_k=fox_g_k,
             fox_b_f=fox_b_f, ab_w_out=ab_w_out, swa_w_in=swa_w_in, swa_g_q=swa_g_q, swa_g_k=swa_g_k,
             swa_sinks=swa_sinks, swa_w_out=swa_w_out, rel_bias=rel_bias)
    n_pool = cache_fox_k.shape[1]
    caches = (cache_mla_ckv, jnp.swapaxes(cache_mla_krope, 2, 3),
              cache_fox_k.reshape(-1, n_pool, PAGE_SIZE * KV_B, HD_B),
              cache_fox_v.reshape(-1, n_pool, PAGE_SIZE * KV_B, HD_B), jnp.swapaxes(cache_fox_logf, 2, 3))
    tabs = _rope_tables()
    x = jnp.concatenate([x_prompt.reshape(MP, D_MODEL), x_sample.reshape(MS, D_MODEL)], axis=0)
    ab_p, ab_s, c_p, c_s = [], [], [], []
    for l in range(DEPTH):
        x = _ffn(x, ffn1_g[l], ffn1_w_gate, ffn1_w_up, ffn1_w_down, l)
        j = l // 2
        if l % 2 == 0:
            x, st_p, st_s = _ab_layer(x, j, tabs, caches, page_table, p)
            ab_p.append(st_p)
            ab_s.append(st_s)
        else:
            x, st_p, st_s = _c_layer(x, j, state_swa_k, state_swa_v, p)
            c_p.append(st_p)
            c_s.append(st_s)
        x = _ffn(x, ffn2_g[l], ffn2_w_gate, ffn2_w_up, ffn2_w_down, l)

    def stack(states, i):
        return jnp.stack([st[i] for st in states], axis=0)

    return (x[:MP].reshape(BATCH, SEQ, D_MODEL), x[MP:].reshape(DEC_BATCH, DEC_SEQ, D_MODEL),
            stack(ab_p, 0), stack(ab_p, 1), stack(ab_p, 2), stack(ab_p, 3), stack(ab_p, 4),
            stack(c_p, 0), stack(c_p, 1),
            stack(ab_s, 0), stack(ab_s, 1), stack(ab_s, 2), stack(ab_s, 3), stack(ab_s, 4),
            stack(c_s, 0), stack(c_s, 1))
```

```python
import functools
import math

import numpy as np
import jax
import jax.numpy as jnp
from jax import lax
from jax.experimental import pallas as pl
from jax.experimental.pallas import tpu as pltpu

f32, bf16, i32 = jnp.float32, jnp.bfloat16, jnp.int32

D_MODEL = 2048
BATCH = 4
SEQ = 2048
DEPTH = 4
DEC_BATCH = 128
DEC_SEQ = 8
PAST_LEN = 8192
PAGE_SIZE = 128
N_PAGES = PAST_LEN // PAGE_SIZE
H_A = 8
Q_LORA = 512
KV_LORA = 512
NOPE_D = 128
ROPE_D = 64
V_D = 128
ROPE_THETA = 10000.0
H_B = 8
KV_B = 2
HD_B = 128
H_C = 32
KV_C = 4
HD_C = 64
WINDOW = 128
NUM_BUCKETS = 32
MAX_DISTANCE = 128
D_FF = 4096
FFN_RES = 0.5
EPS = 1e-6
NEG_INF = -1e30
MLA_SCALE = (NOPE_D + ROPE_D) ** -0.5
FOX_SCALE = HD_B ** -0.5
C_SCALE = HD_C ** -0.5
G_B = H_B // KV_B
G_C = H_C // KV_C
MAX_EXACT = NUM_BUCKETS // 2

MP = BATCH * SEQ
MS = DEC_BATCH * DEC_SEQ
MT = MP + MS
IN_AB_PAD = 2688
QH = 256

LANES = 128
MXU_DIM = 256
VMEM_LIMIT = 56 * 1024 * 1024

NT = (((1,), (1,)), ((), ()))


def _cparams(sem, vmem_limit=VMEM_LIMIT):
    return pltpu.CompilerParams(dimension_semantics=sem, vmem_limit_bytes=vmem_limit)


def _rms(x, g, eps=EPS):
    return (x * lax.rsqrt(jnp.mean(x * x, axis=-1, keepdims=True) + eps)) * g


def _split3(x):
    hi = x.astype(bf16)
    r1 = x - hi.astype(f32)
    mid = r1.astype(bf16)
    lo = (r1 - mid.astype(f32)).astype(bf16)
    return hi, mid, lo


FFN_TM = 1024
FFN_TF = 256
FFN_VMEM_LIMIT = (4 * FFN_TM * D_MODEL * 4 + FFN_TM * D_MODEL * 2 + 6 * D_MODEL * FFN_TF * 4
                  + 12 * FFN_TM * FFN_TF * 4)


def _ffn_body(x_ref, g_ref, wg_ref, wu_ref, wd_ref, o_ref, n_ref):
    @pl.when(pl.program_id(1) == 0)
    def _():
        x = x_ref[...]
        n_ref[...] = _rms(x, g_ref[...]).astype(bf16)
        o_ref[...] = x

    n = n_ref[...]
    gate = jnp.dot(n, wg_ref[...].astype(bf16), preferred_element_type=f32)
    up = jnp.dot(n, wu_ref[...].astype(bf16), preferred_element_type=f32)
    h = (gate / (1.0 + jnp.exp(-gate))) * up * FFN_RES
    o_ref[...] += jnp.dot(h.astype(bf16), wd_ref[...].astype(bf16), preferred_element_type=f32)


def _ffn(x, g, wg, wu, wd, l):
    m = x.shape[0]
    return pl.pallas_call(
        _ffn_body,
        grid=(m // FFN_TM, D_FF // FFN_TF),
        in_specs=[
            pl.BlockSpec((FFN_TM, D_MODEL), lambda i, f: (i, 0)),
            pl.BlockSpec((1, D_MODEL), lambda i, f: (0, 0)),
            pl.BlockSpec((None, D_MODEL, FFN_TF), lambda i, f: (l, 0, f)),
            pl.BlockSpec((None, D_MODEL, FFN_TF), lambda i, f: (l, 0, f)),
            pl.BlockSpec((None, FFN_TF, D_MODEL), lambda i, f: (l, f, 0)),
        ],
        out_specs=pl.BlockSpec((FFN_TM, D_MODEL), lambda i, f: (i, 0)),
        out_shape=jax.ShapeDtypeStruct((m, D_MODEL), f32),
        scratch_shapes=[pltpu.VMEM((FFN_TM, D_MODEL), bf16)],
        compiler_params=_cparams(("parallel", "arbitrary"), FFN_VMEM_LIMIT),
        name="ffn_half",
    )(x, g.reshape(1, D_MODEL), wg, wu, wd)


OUT_TM = 512


def _outproj_body(x_ref, o_ref, w_ref, y_ref):
    y_ref[...] = x_ref[...] + jnp.dot(o_ref[...], w_ref[...], preferred_element_type=f32)


def _outproj(x, o, w):
    m = x.shape[0]
    return pl.pallas_call(
        _outproj_body,
        grid=(m // OUT_TM,),
        in_specs=[
            pl.BlockSpec((OUT_TM, D_MODEL), lambda i: (i, 0)),
            pl.BlockSpec((OUT_TM, D_MODEL), lambda i: (i, 0)),
            pl.BlockSpec((D_MODEL, D_MODEL), lambda i: (0, 0)),
        ],
        out_specs=pl.BlockSpec((OUT_TM, D_MODEL), lambda i: (i, 0)),
        out_shape=jax.ShapeDtypeStruct((m, D_MODEL), f32),
        compiler_params=_cparams(("parallel",)),
        name="out_proj",
    )(x, o, w)


AB_TM = 256


def _half_masks(shape):
    lane = lax.broadcasted_iota(i32, shape, 1)
    return lane < 64, (lane % 64) < 32


def _norm_rope_tile(t, g128, cosv, sinv, lo, first):
    ss = t * t
    s_lo = jnp.sum(jnp.where(lo, ss, 0.0), axis=-1, keepdims=True)
    s_hi = jnp.sum(jnp.where(lo, 0.0, ss), axis=-1, keepdims=True)
    r = jnp.where(lo, lax.rsqrt(s_lo / 64.0 + EPS), lax.rsqrt(s_hi / 64.0 + EPS))
    y = (t * r) * g128
    other = jnp.where(first, pltpu.roll(y, 96, 1), pltpu.roll(y, 32, 1))
    return y * cosv + other * sinv


def _abproj_body(x_ref, mg_ref, win_ref, gql_ref, wqu_ref, gkv_ref, gqn_ref, gqr_ref, gkr_ref,
                 gfq_ref, gfk_ref, bf_ref, cos_ref, sin_ref,
                 qmla_ref, ckv_ref, krope_ref, kr128_ref, fq_ref, fk_ref, fv_ref, logf_ref):
    n = _rms(x_ref[...], mg_ref[...]).astype(bf16)
    y = jnp.dot(n, win_ref[...], preferred_element_type=f32)
    cosv, sinv = cos_ref[...], sin_ref[...]
    lo, first = _half_masks(cosv.shape)

    ql = _rms(y[:, 0:Q_LORA], gql_ref[...]).astype(bf16)
    q = jnp.dot(ql, wqu_ref[...], preferred_element_type=f32)
    for h in range(H_A):
        t = _rms(q[:, h * NOPE_D:(h + 1) * NOPE_D], gqn_ref[...]) * MLA_SCALE
        qmla_ref[:, h * QH:h * QH + NOPE_D] = t.astype(bf16)
    base = H_A * NOPE_D
    for p in range(H_A // 2):
        t = _norm_rope_tile(q[:, base + p * LANES:base + (p + 1) * LANES], gqr_ref[...],
                            cosv, sinv, lo, first) * MLA_SCALE
        even = jnp.where(lo, t, 0.0)
        odd = jnp.where(lo, pltpu.roll(t, 64, 1), 0.0)
        qmla_ref[:, (2 * p) * QH + NOPE_D:(2 * p + 1) * QH] = even.astype(bf16)
        qmla_ref[:, (2 * p + 1) * QH + NOPE_D:(2 * p + 2) * QH] = odd.astype(bf16)

    ckv_ref[...] = _rms(y[:, 512:1024], gkv_ref[...])

    for h in range(H_B):
        t = _rms(y[:, 1024 + h * HD_B:1024 + (h + 1) * HD_B], gfq_ref[...]) * FOX_SCALE
        fq_ref[:, h * HD_B:(h + 1) * HD_B] = t.astype(bf16)
    for k in range(KV_B):
        fk_ref[:, k * HD_B:(k + 1) * HD_B] = _rms(y[:, 2048 + k * HD_B:2048 + (k + 1) * HD_B], gfk_ref[...])
    fv_ref[...] = y[:, 2304:2560]

    t = y[:, 2560:2688]
    kr = _norm_rope_tile(t, gkr_ref[...], cosv, sinv, lo, first)
    krope_ref[...] = kr[:, 0:ROPE_D]
    kr128_ref[...] = jnp.where(lo, kr, 0.0).astype(bf16)
    z = t + bf_ref[...]
    ls = jnp.minimum(z, 0.0) - jnp.log(1.0 + jnp.exp(-jnp.abs(z)))
    logf_ref[...] = ls[:, 64:64 + H_B]


def _abproj(x, mg, win, gql, wqu, gkv, gqn, gqr128, gkr128, gfq, gfk, bf128, cos128, sin128):
    m = x.shape[0]
    row = lambda w: pl.BlockSpec((AB_TM, w), lambda i: (i, 0))
    full = lambda a: pl.BlockSpec(a.shape, lambda i: (0,) * a.ndim)
    ins = [x, mg, win, gql, wqu, gkv, gqn, gqr128, gkr128, gfq, gfk, bf128, cos128, sin128]
    in_specs = [row(D_MODEL)] + [full(a) for a in ins[1:12]] + [row(LANES), row(LANES)]
    outs = [(H_A * QH, bf16), (KV_LORA, f32), (ROPE_D, f32), (LANES, bf16), (H_B * HD_B, bf16),
            (KV_B * HD_B, f32), (KV_B * HD_B, f32), (H_B, f32)]
    return pl.pallas_call(
        _abproj_body,
        grid=(m // AB_TM,),
        in_specs=in_specs,
        out_specs=[row(w) for w, _ in outs],
        out_shape=[jax.ShapeDtypeStruct((m, w), dt) for w, dt in outs],
        compiler_params=_cparams(("parallel",)),
        name="ab_proj",
    )(*ins)


KV_TM = 512


def _kvup_body(ckv_ref, kr128_ref, wuk_ref, wuv_ref, gkn_ref, kmla_ref, va_ref):
    c = ckv_ref[...].astype(bf16)
    kk = jnp.dot(c, wuk_ref[...], preferred_element_type=f32)
    kr = kr128_ref[...]
    for h in range(H_A):
        kmla_ref[:, h * QH:h * QH + NOPE_D] = _rms(kk[:, h * NOPE_D:(h + 1) * NOPE_D], gkn_ref[...]).astype(bf16)
        kmla_ref[:, h * QH + NOPE_D:(h + 1) * QH] = kr
    va_ref[...] = jnp.dot(c, wuv_ref[...], preferred_element_type=f32).astype(bf16)


def _kvup(ckv, kr128, wuk, wuv, gkn):
    m = MP
    row = lambda w: pl.BlockSpec((KV_TM, w), lambda i: (i, 0))
    full = lambda a: pl.BlockSpec(a.shape, lambda i: (0,) * a.ndim)
    return pl.pallas_call(
        _kvup_body,
        grid=(m // KV_TM,),
        in_specs=[row(KV_LORA), row(LANES), full(wuk), full(wuv), full(gkn)],
        out_specs=[row(H_A * QH), row(H_A * V_D)],
        out_shape=[jax.ShapeDtypeStruct((m, H_A * QH), bf16), jax.ShapeDtypeStruct((m, H_A * V_D), bf16)],
        compiler_params=_cparams(("parallel",)),
        name="mla_kv_up",
    )(ckv, kr128, wuk, wuv, gkn)


CS_CHUNK = MXU_DIM


def _tri_ones(n):
    r = lax.broadcasted_iota(i32, (n, n), 0)
    c = lax.broadcasted_iota(i32, (n, n), 1)
    return jnp.where(r <= c, 1.0, 0.0).astype(bf16)


def _cumsum_chunks(parts_of, n_chunks, write, carry):
    tri = _tri_ones(CS_CHUNK)
    for ch in range(n_chunks):
        cs = carry
        for p in parts_of(ch):
            cs = cs + jnp.dot(p, tri, preferred_element_type=f32)
        write(ch, cs)
        carry = cs[:, CS_CHUNK - 1:CS_CHUNK]
    return carry


def _cumsum_body(x_ref, o_ref):
    def parts_of(ch):
        return _split3(x_ref[:, ch * CS_CHUNK:(ch + 1) * CS_CHUNK])

    def write(ch, cs):
        o_ref[:, ch * CS_CHUNK:(ch + 1) * CS_CHUNK] = cs

    _cumsum_chunks(parts_of, SEQ // CS_CHUNK, write, jnp.zeros((H_B, 1), f32))


def _cumsum_prompt(lft):
    return pl.pallas_call(
        _cumsum_body,
        grid=(BATCH,),
        in_specs=[pl.BlockSpec((None, H_B, SEQ), lambda b: (b, 0, 0))],
        out_specs=pl.BlockSpec((None, H_B, SEQ), lambda b: (b, 0, 0)),
        out_shape=jax.ShapeDtypeStruct((BATCH, H_B, SEQ), f32),
        compiler_params=_cparams(("parallel",)),
        name="fox_cumsum_prompt",
    )(lft)


FL_T = 512


def _flash_body(*refs, NH, G, DQ, DV, has_bias, nk):
    if has_bias:
        q_ref, k_ref, v_ref, fq_ref, fk_ref, o_ref, m_scr, acc_scr = refs
    else:
        q_ref, k_ref, v_ref, o_ref, m_scr, acc_scr = refs
    assert DV == LANES
    i = pl.program_id(2)
    j = pl.program_id(3)

    @pl.when(j == 0)
    def _():
        m_scr[...] = jnp.full(m_scr.shape, NEG_INF, f32)
        acc_scr[...] = jnp.zeros(acc_scr.shape, f32)

    def tile(diagonal):
        if diagonal:
            mask = (lax.broadcasted_iota(i32, (FL_T, FL_T), 1) <= lax.broadcasted_iota(i32, (FL_T, FL_T), 0))
        ones = jnp.ones((FL_T, LANES), bf16)
        for kv in range(NH):
            k = k_ref[:, kv * DQ:(kv + 1) * DQ].astype(bf16)
            v1 = jnp.concatenate([v_ref[:, kv * DV:(kv + 1) * DV].astype(bf16), ones], axis=1)
            for g in range(G):
                h = kv * G + g
                s = lax.dot_general(q_ref[:, h * DQ:(h + 1) * DQ], k, NT, preferred_element_type=f32)
                if has_bias:
                    s = s + (fq_ref[:, h:h + 1] - fk_ref[h:h + 1, :])
                if diagonal:
                    s = jnp.where(mask, s, NEG_INF)
                m_prev = m_scr[h]
                m_new = jnp.maximum(m_prev, jnp.max(s, axis=-1, keepdims=True))
                alpha = jnp.exp(m_prev - m_new)
                p = jnp.exp(s - jnp.concatenate([m_new] * (FL_T // LANES), axis=1))
                acc_scr[h] = (jnp.concatenate([alpha, alpha], axis=1) * acc_scr[h]
                              + jnp.dot(p.astype(bf16), v1, preferred_element_type=f32))
                m_scr[h] = m_new

    @pl.when(j < i)
    def _():
        tile(False)

    @pl.when(j == i)
    def _():
        tile(True)

    @pl.when(j == nk - 1)
    def _():
        for h in range(NH * G):
            acc = acc_scr[h]
            o_ref[:, h * DV:(h + 1) * DV] = (acc[:, 0:DV] / acc[:, DV:2 * DV]).astype(o_ref.dtype)


def _flash_prompt(q, k, v, n_kv, NH, G, DQ, DV, fq=None, fk=None):
    nq = SEQ // FL_T
    has_bias = fq is not None
    qmap = lambda b, h, i, j: (b * nq + i, h)
    kmap = lambda b, h, i, j: (b * nq + jnp.minimum(i, j), h)
    in_specs = [pl.BlockSpec((FL_T, NH * G * DQ), qmap), pl.BlockSpec((FL_T, NH * DQ), kmap),
                pl.BlockSpec((FL_T, NH * DV), kmap)]
    ins = [q, k, v]
    if has_bias:
        assert NH * G == H_B
        in_specs += [pl.BlockSpec((FL_T, H_B), lambda b, h, i, j: (b * nq + i, 0)),
                     pl.BlockSpec((None, H_B, FL_T), lambda b, h, i, j: (b, 0, jnp.minimum(i, j)))]
        ins += [fq, fk]
    nh = NH * G
    return pl.pallas_call(
        functools.partial(_flash_body, NH=NH, G=G, DQ=DQ, DV=DV, has_bias=has_bias, nk=nq),
        grid=(BATCH, n_kv // NH, nq, nq),
        in_specs=in_specs,
        out_specs=pl.BlockSpec((FL_T, nh * DV), qmap),
        out_shape=jax.ShapeDtypeStruct((MP, n_kv * G * DV), bf16),
        scratch_shapes=[pltpu.VMEM((nh, FL_T, LANES), f32), pltpu.VMEM((nh, FL_T, 2 * DV), f32)],
        compiler_params=_cparams(("parallel", "parallel", "parallel", "arbitrary")),
        name="flash_prompt_bias" if has_bias else "flash_prompt",
    )(*ins)


def _page_copies(pt_ref, layer, b, c, slot, pages, specs):
    out = []
    for g in range(pages):
        page = pt_ref[b, c * pages + g]
        for hbm_ref, buf_ref, sem_ref, rows in specs:
            if rows:
                dst = buf_ref.at[slot, pl.ds(g * rows, rows), :]
            else:
                dst = buf_ref.at[slot, :, pl.ds(g * PAGE_SIZE, PAGE_SIZE)]
            out.append(pltpu.make_async_copy(hbm_ref.at[layer, page], dst, sem_ref.at[slot]))
    return out


def _gather_wait(pt_ref, layer, n_chunks, pages, specs):
    b, c = pl.program_id(0), pl.program_id(1)
    n = b * n_chunks + c
    slot = n % 2

    @pl.when(n == 0)
    def _():
        for cp in _page_copies(pt_ref, layer, 0, 0, 0, pages, specs):
            cp.start()

    for cp in _page_copies(pt_ref, layer, b, c, slot, pages, specs):
        cp.wait()
    return slot


def _gather_prefetch(pt_ref, layer, n_chunks, pages, specs):
    n = pl.program_id(0) * n_chunks + pl.program_id(1)
    nxt = (n + 1) % (pl.num_programs(0) * n_chunks)
    for cp in _page_copies(pt_ref, layer, nxt // n_chunks, nxt % n_chunks, (n + 1) % 2, pages, specs):
        cp.start()


def _gather_drain(pt_ref, layer, n_chunks, pages, specs):
    n = pl.program_id(0) * n_chunks + pl.program_id(1)

    @pl.when(n == pl.num_programs(0) * n_chunks - 1)
    def _():
        for cp in _page_copies(pt_ref, layer, 0, 0, (n + 1) % 2, pages, specs):
            cp.wait()


def _softmax_update(s, v, m_ref, l_ref, acc_ref):
    m_prev = m_ref[...]
    m_new = jnp.maximum(m_prev, jnp.max(s, axis=-1, keepdims=True))
    alpha = jnp.exp(m_prev - m_new)
    p = jnp.exp(s - m_new)
    l_ref[...] = alpha * l_ref[...] + jnp.sum(p, axis=-1, keepdims=True)
    acc_ref[...] = alpha * acc_ref[...] + jnp.dot(p.astype(bf16), v, preferred_element_type=f32)
    m_ref[...] = m_new


def _new_key_mask(rows):
    t = lax.broadcasted_iota(i32, (rows, LANES), 0) % DEC_SEQ
    col = lax.broadcasted_iota(i32, (rows, LANES), 1)
    return (col < DEC_SEQ) & (col <= t)


LF_ROWS = PAST_LEN + CS_CHUNK
LF_SB = 16


def _lf_copies(pt_ref, cache_ref, buf, sem, layer, step, slot, start):
    def one_seq(sl, carry):
        for p in range(N_PAGES):
            cp = pltpu.make_async_copy(cache_ref.at[layer, pt_ref[step * LF_SB + sl, p]],
                                       buf.at[slot, pl.ds(sl * H_B, H_B), pl.ds(p * PAGE_SIZE, PAGE_SIZE)],
                                       sem.at[slot])
            if start:
                cp.start()
            else:
                cp.wait()
        return carry
    lax.fori_loop(0, LF_SB, one_seq, 0)


def _lf_sample_body(pt_ref, cache_ref, lfn_ref, fk_ref, fn_ref, buf, sem, *, layer):
    n = pl.program_id(0)
    slot = n % 2

    @pl.when(n == 0)
    def _():
        _lf_copies(pt_ref, cache_ref, buf, sem, layer, 0, 0, True)

    @pl.when(n + 1 < pl.num_programs(0))
    def _():
        _lf_copies(pt_ref, cache_ref, buf, sem, layer, n + 1, (n + 1) % 2, True)

    _lf_copies(pt_ref, cache_ref, buf, sem, layer, n, slot, False)
    buf[slot, :, PAST_LEN:LF_ROWS] = lfn_ref[...]

    def parts_of(ch):
        return _split3(buf[slot, :, ch * CS_CHUNK:(ch + 1) * CS_CHUNK])

    def write(ch, cs):
        if ch < PAST_LEN // CS_CHUNK:
            fk_ref[:, ch * CS_CHUNK:(ch + 1) * CS_CHUNK] = cs
        else:
            fn_ref[...] = cs[:, 0:LANES]

    _cumsum_chunks(parts_of, LF_ROWS // CS_CHUNK, write, jnp.zeros((LF_SB * H_B, 1), f32))


def _lf_sample(page_table, cache_logf_t, lf_new_t, layer):
    rows = LF_SB * H_B
    grid_spec = pltpu.PrefetchScalarGridSpec(
        num_scalar_prefetch=1,
        grid=(DEC_BATCH // LF_SB,),
        in_specs=[pl.BlockSpec(memory_space=pl.ANY),
                  pl.BlockSpec((rows, CS_CHUNK), lambda n, pt: (n, 0))],
        out_specs=[pl.BlockSpec((rows, PAST_LEN), lambda n, pt: (n, 0)),
                   pl.BlockSpec((rows, LANES), lambda n, pt: (n, 0))],
        scratch_shapes=[pltpu.VMEM((2, rows, LF_ROWS), f32), pltpu.SemaphoreType.DMA((2,))],
    )
    return pl.pallas_call(
        functools.partial(_lf_sample_body, layer=layer),
        grid_spec=grid_spec,
        out_shape=[jax.ShapeDtypeStruct((DEC_BATCH * H_B, PAST_LEN), f32),
                   jax.ShapeDtypeStruct((DEC_BATCH * H_B, LANES), f32)],
        compiler_params=_cparams(("arbitrary",)),
        name="fox_logf_sample",
    )(page_table, cache_logf_t, lf_new_t)


MLA_PAGES = 32
MLA_KEYS = MLA_PAGES * PAGE_SIZE
MLA_CHUNKS = N_PAGES // MLA_PAGES
DEC_SUB = MXU_DIM
QROWS = H_A * DEC_SEQ


def _mla_scores(a_ref, c_bf, rope):
    r = lax.dot_general(a_ref[...], c_bf, NT, preferred_element_type=f32)
    out = []
    for h in range(H_A):
        kk = r[h * NOPE_D:(h + 1) * NOPE_D, :]
        inv = lax.rsqrt(jnp.sum(kk * kk, axis=0, keepdims=True) / NOPE_D + EPS)
        lo_, hi_ = H_A * NOPE_D + h * DEC_SEQ, H_A * NOPE_D + (h + 1) * DEC_SEQ
        out.append(r[lo_:hi_, :] * inv)
    return jnp.concatenate(out, axis=0) + rope


def _mla_decode_body(pt_ref, ckv_hbm, kr_hbm, q_ref, cn_ref, krn_ref, wukt_ref, wuv_ref, gkn_ref, o_ref,
                     cbuf, krbuf, csem, krsem, a_scr, qr_scr, s_scr, cbf_scr, cn_pad, krn_pad, m_scr, l_scr, acc_scr,
                     *, layer):
    b, c = pl.program_id(0), pl.program_id(1)
    nk = H_A * NOPE_D

    @pl.when((b == 0) & (c == 0))
    def _():
        a_scr[0:nk, :] = wukt_ref[...]
        cn_pad[...] = jnp.zeros(cn_pad.shape, f32)
        krn_pad[...] = jnp.zeros(krn_pad.shape, f32)

    gather = (pt_ref, layer, MLA_CHUNKS, MLA_PAGES, [(ckv_hbm, cbuf, csem, PAGE_SIZE), (kr_hbm, krbuf, krsem, 0)])
    _gather_prefetch(*gather)
    slot = _gather_wait(*gather)

    @pl.when(c == 0)
    def _():
        q = q_ref[...]
        wq, qr_rows = [], []
        for h in range(H_A):
            qg = (q[:, h * QH:h * QH + NOPE_D] * gkn_ref[...]).astype(bf16)
            wq.append(jnp.dot(qg, wukt_ref[h * NOPE_D:(h + 1) * NOPE_D, :], preferred_element_type=f32))
            qr_rows.append(q[:, h * QH + NOPE_D:h * QH + NOPE_D + ROPE_D])
        a_scr[nk:nk + QROWS, :] = jnp.concatenate(wq, axis=0).astype(bf16)
        qr_scr[...] = jnp.concatenate(qr_rows, axis=0).astype(bf16)
        m_scr[...] = jnp.full(m_scr.shape, NEG_INF, f32)
        l_scr[...] = jnp.zeros(l_scr.shape, f32)
        acc_scr[...] = jnp.zeros(acc_scr.shape, f32)

    qr = qr_scr[...]
    for sub in range(MLA_KEYS // DEC_SUB):
        keys = pl.ds(sub * DEC_SUB, DEC_SUB)
        rope = jnp.dot(qr, krbuf[slot, :, keys].astype(bf16), preferred_element_type=f32)
        c_bf = cbuf[slot, keys, :].astype(bf16)
        cbf_scr[sub * DEC_SUB:(sub + 1) * DEC_SUB, :] = c_bf
        s_scr[:, sub * DEC_SUB:(sub + 1) * DEC_SUB] = _mla_scores(a_scr, c_bf, rope)
    _softmax_update(s_scr[...], cbf_scr[...], m_scr, l_scr, acc_scr)

    @pl.when(c == MLA_CHUNKS - 1)
    def _():
        cn_pad[0:DEC_SEQ, :] = cn_ref[...]
        krn_pad[0:DEC_SEQ, :] = krn_ref[...]
        cn = cn_pad[...].astype(bf16)
        rope = lax.dot_general(qr, krn_pad[...].astype(bf16), NT, preferred_element_type=f32)
        s = _mla_scores(a_scr, cn, rope)
        s = jnp.where(_new_key_mask(QROWS), s, NEG_INF)
        _softmax_update(s, cn, m_scr, l_scr, acc_scr)
        o_lat = (acc_scr[...] / l_scr[...]).astype(bf16)
        for h in range(H_A):
            o_ref[:, h * V_D:(h + 1) * V_D] = jnp.dot(
                o_lat[h * DEC_SEQ:(h + 1) * DEC_SEQ, :], wuv_ref[:, h * V_D:(h + 1) * V_D],
                preferred_element_type=f32).astype(o_ref.dtype)

    _gather_drain(*gather)


def _mla_decode(page_table, cache_ckv, cache_krope, q_s, c_new, kr_new, wukt, wuv, gkn, layer):
    seq = lambda w: pl.BlockSpec((None, DEC_SEQ, w), lambda b, c, pt: (b, 0, 0))
    full = lambda a: pl.BlockSpec(a.shape, lambda b, c, pt: (0,) * a.ndim)
    grid_spec = pltpu.PrefetchScalarGridSpec(
        num_scalar_prefetch=1,
        grid=(DEC_BATCH, MLA_CHUNKS),
        in_specs=[pl.BlockSpec(memory_space=pl.ANY), pl.BlockSpec(memory_space=pl.ANY),
                  seq(H_A * QH), seq(KV_LORA), seq(ROPE_D), full(wukt), full(wuv), full(gkn)],
        out_specs=seq(H_A * V_D),
        scratch_shapes=[
            pltpu.VMEM((2, MLA_KEYS, KV_LORA), f32), pltpu.VMEM((2, ROPE_D, MLA_KEYS), f32),
            pltpu.SemaphoreType.DMA((2,)), pltpu.SemaphoreType.DMA((2,)),
            pltpu.VMEM((H_A * NOPE_D + QROWS, KV_LORA), bf16), pltpu.VMEM((QROWS, ROPE_D), bf16),
            pltpu.VMEM((QROWS, MLA_KEYS), f32), pltpu.VMEM((MLA_KEYS, KV_LORA), bf16),
            pltpu.VMEM((LANES, KV_LORA), f32), pltpu.VMEM((LANES, ROPE_D), f32),
            pltpu.VMEM((QROWS, 1), f32), pltpu.VMEM((QROWS, 1), f32), pltpu.VMEM((QROWS, KV_LORA), f32),
        ],
    )
    return pl.pallas_call(
        functools.partial(_mla_decode_body, layer=layer),
        grid_spec=grid_spec,
        out_shape=jax.ShapeDtypeStruct((DEC_BATCH, DEC_SEQ, H_A * V_D), f32),
        compiler_params=_cparams(("arbitrary", "arbitrary")),
        name="mla_decode",
    )(page_table, cache_ckv, cache_krope, q_s, c_new, kr_new, wukt, wuv, gkn)


GROWS = G_B * DEC_SEQ
FOX_PAGES = 64
FOX_KEYS = FOX_PAGES * PAGE_SIZE
FOX_CHUNKS = N_PAGES // FOX_PAGES


def _fox_decode_body(pt_ref, k_hbm, v_hbm, q_ref, kn_ref, vn_ref, fq_ref, fk_ref, fn_ref, o_ref,
                     kbuf, vbuf, ksem, vsem, kn_pad, vn_pad, q_scr, m_scr, l_scr, acc_scr, *, layer):
    b, c = pl.program_id(0), pl.program_id(1)

    @pl.when((b == 0) & (c == 0))
    def _():
        kn_pad[...] = jnp.zeros(kn_pad.shape, f32)
        vn_pad[...] = jnp.zeros(vn_pad.shape, f32)

    gather = (pt_ref, layer, FOX_CHUNKS, FOX_PAGES,
              [(k_hbm, kbuf, ksem, PAGE_SIZE * KV_B), (v_hbm, vbuf, vsem, PAGE_SIZE * KV_B)])
    _gather_prefetch(*gather)
    slot = _gather_wait(*gather)

    @pl.when(c == 0)
    def _():
        q = q_ref[...]
        q_scr[...] = jnp.concatenate([q[:, h * HD_B:(h + 1) * HD_B] for h in range(H_B)], axis=0).astype(bf16)
        m_scr[...] = jnp.full(m_scr.shape, NEG_INF, f32)
        l_scr[...] = jnp.zeros(l_scr.shape, f32)
        acc_scr[...] = jnp.zeros(acc_scr.shape, f32)

    def bias(fk):
        rows = [fq_ref[h * DEC_SEQ:(h + 1) * DEC_SEQ, :] - fk[h:h + 1, :] for h in range(H_B)]
        return jnp.concatenate(rows, axis=0)

    def attend(kv, k_bf, v_bf, bias_kv, mask):
        rows = slice(kv * GROWS, (kv + 1) * GROWS)
        s = lax.dot_general(q_scr[rows, :], k_bf, NT, preferred_element_type=f32) + bias_kv
        if mask is not None:
            s = jnp.where(mask, s, NEG_INF)
        _softmax_update(s, v_bf, m_scr.at[rows], l_scr.at[rows], acc_scr.at[rows])

    bias_past = bias(fk_ref[...])
    for kv in range(KV_B):
        sel = pl.ds(kv, FOX_KEYS, stride=KV_B)
        attend(kv, kbuf[slot, sel, :].astype(bf16), vbuf[slot, sel, :].astype(bf16),
               bias_past[kv * GROWS:(kv + 1) * GROWS], None)

    @pl.when(c == FOX_CHUNKS - 1)
    def _():
        bias_new = bias(fn_ref[...])
        mask = _new_key_mask(GROWS)
        for kv in range(KV_B):
            kn_pad[0:DEC_SEQ, :] = kn_ref[:, kv * HD_B:(kv + 1) * HD_B]
            vn_pad[0:DEC_SEQ, :] = vn_ref[:, kv * HD_B:(kv + 1) * HD_B]
            attend(kv, kn_pad[...].astype(bf16), vn_pad[...].astype(bf16),
                   bias_new[kv * GROWS:(kv + 1) * GROWS], mask)
        o = acc_scr[...] / l_scr[...]
        for h in range(H_B):
            o_ref[:, h * HD_B:(h + 1) * HD_B] = o[h * DEC_SEQ:(h + 1) * DEC_SEQ, :].astype(o_ref.dtype)

    _gather_drain(*gather)


def _fox_decode(page_table, cache_k, cache_v, q_s, k_new, v_new, fq_col, fk_past, fn_new, layer):
    seq = lambda w: pl.BlockSpec((None, DEC_SEQ, w), lambda b, c, pt: (b, 0, 0))
    grid_spec = pltpu.PrefetchScalarGridSpec(
        num_scalar_prefetch=1,
        grid=(DEC_BATCH, FOX_CHUNKS),
        in_specs=[pl.BlockSpec(memory_space=pl.ANY), pl.BlockSpec(memory_space=pl.ANY),
                  seq(H_B * HD_B), seq(KV_B * HD_B), seq(KV_B * HD_B),
                  pl.BlockSpec((None, H_B * DEC_SEQ, 1), lambda b, c, pt: (b, 0, 0)),
                  pl.BlockSpec((None, H_B, FOX_KEYS), lambda b, c, pt: (b, 0, c)),
                  pl.BlockSpec((None, H_B, LANES), lambda b, c, pt: (b, 0, 0))],
        out_specs=seq(H_B * HD_B),
        scratch_shapes=[
            pltpu.VMEM((2, FOX_KEYS * KV_B, HD_B), f32), pltpu.VMEM((2, FOX_KEYS * KV_B, HD_B), f32),
            pltpu.SemaphoreType.DMA((2,)), pltpu.SemaphoreType.DMA((2,)),
            pltpu.VMEM((LANES, HD_B), f32), pltpu.VMEM((LANES, HD_B), f32),
            pltpu.VMEM((H_B * DEC_SEQ, HD_B), bf16),
            pltpu.VMEM((H_B * DEC_SEQ, 1), f32), pltpu.VMEM((H_B * DEC_SEQ, 1), f32),
            pltpu.VMEM((H_B * DEC_SEQ, HD_B), f32),
        ],
    )
    return pl.pallas_call(
        functools.partial(_fox_decode_body, layer=layer),
        grid_spec=grid_spec,
        out_shape=jax.ShapeDtypeStruct((DEC_BATCH, DEC_SEQ, H_B * HD_B), f32),
        compiler_params=_cparams(("arbitrary", "arbitrary")),
        name="fox_decode",
    )(page_table, cache_k, cache_v, q_s, k_new, v_new, fq_col, fk_past, fn_new)


C_TM = 256
C_IN = (H_C + 2 * KV_C) * HD_C


def _norm_pair_tile(t, g128, lo):
    ss = t * t
    s_lo = jnp.sum(jnp.where(lo, ss, 0.0), axis=-1, keepdims=True)
    s_hi = jnp.sum(jnp.where(lo, 0.0, ss), axis=-1, keepdims=True)
    r = jnp.where(lo, lax.rsqrt(s_lo / 64.0 + EPS), lax.rsqrt(s_hi / 64.0 + EPS))
    return (t * r) * g128


def _cproj_body(x_ref, mg_ref, win_ref, gq_ref, gk_ref, q_ref, k_ref, v_ref):
    n = _rms(x_ref[...], mg_ref[...]).astype(bf16)
    y = jnp.dot(n, win_ref[...], preferred_element_type=f32)
    lo, _ = _half_masks((C_TM, LANES))
    nq = H_C * HD_C
    for p in range(nq // LANES):
        q_ref[:, p * LANES:(p + 1) * LANES] = (
            _norm_pair_tile(y[:, p * LANES:(p + 1) * LANES], gq_ref[...], lo) * C_SCALE).astype(bf16)
    for p in range(KV_C * HD_C // LANES):
        k_ref[:, p * LANES:(p + 1) * LANES] = _norm_pair_tile(y[:, nq + p * LANES:nq + (p + 1) * LANES], gk_ref[...], lo)
    v_ref[...] = y[:, nq + KV_C * HD_C:]


def _cproj(x, mg, win, gq128, gk128):
    m = x.shape[0]
    row = lambda w: pl.BlockSpec((C_TM, w), lambda i: (i, 0))
    full = lambda a: pl.BlockSpec(a.shape, lambda i: (0,) * a.ndim)
    return pl.pallas_call(
        _cproj_body,
        grid=(m // C_TM,),
        in_specs=[row(D_MODEL), full(mg), full(win), full(gq128), full(gk128)],
        out_specs=[row(H_C * HD_C), row(KV_C * HD_C), row(KV_C * HD_C)],
        out_shape=[jax.ShapeDtypeStruct((m, H_C * HD_C), bf16), jax.ShapeDtypeStruct((m, KV_C * HD_C), f32),
                   jax.ShapeDtypeStruct((m, KV_C * HD_C), f32)],
        compiler_params=_cparams(("parallel",)),
        name="swa_proj",
    )(x, mg, win, gq128, gk128)


def _t5_bucket_np(dist):
    n = np.maximum(dist, 0)
    nf = np.maximum(n, 1).astype(np.float32)
    large = MAX_EXACT + (np.log(nf / np.float32(MAX_EXACT)) / np.float32(math.log(MAX_DISTANCE / MAX_EXACT))
                         * np.float32(NUM_BUCKETS - MAX_EXACT)).astype(np.int32)
    return np.where(n < MAX_EXACT, n, np.minimum(large, NUM_BUCKETS - 1)).astype(np.int32)


def _window_bucket_and_mask(q_loc, n_keys):
    dist = q_loc[:, None] - np.arange(n_keys)[None, :]
    return np.where((dist >= 0) & (dist < WINDOW), _t5_bucket_np(dist), -1).astype(np.int32)


def _fill_bias(rel_ref, bucket, bias_scr):
    def one_head(h, carry):
        acc = jnp.full(bucket.shape, NEG_INF, f32)
        for bk in range(NUM_BUCKETS):
            acc = jnp.where(bucket == bk, rel_ref[bk, h], acc)
        bias_scr[h] = acc
        return carry
    lax.fori_loop(0, H_C, one_head, 0)


def _sink_softmax_pv(s, sink, v_bf):
    m = jnp.maximum(jnp.max(s, axis=-1, keepdims=True), sink)
    e = jnp.exp(s - m)
    den = jnp.sum(e, axis=-1, keepdims=True) + jnp.exp(sink - m)
    return jnp.dot(e.astype(bf16), v_bf, preferred_element_type=f32) / den


def _swa_prompt_body(rel_ref, sink_ref, bucket_ref, q_ref, kp_ref, kc_ref, vp_ref, vc_ref, o_ref, bias_scr):
    b, n = pl.program_id(0), pl.program_id(1)

    @pl.when((b == 0) & (n == 0))
    def _():
        _fill_bias(rel_ref, bucket_ref[...], bias_scr)

    def block(first):
        kb = jnp.concatenate([kp_ref[...], kc_ref[...]], axis=0).astype(bf16)
        vb = jnp.concatenate([vp_ref[...], vc_ref[...]], axis=0).astype(bf16)
        if first:
            has_prev = lax.broadcasted_iota(i32, (WINDOW, 2 * WINDOW), 1) >= WINDOW
        for kv in range(KV_C):
            k_h = kb[:, kv * HD_C:(kv + 1) * HD_C]
            v_h = vb[:, kv * HD_C:(kv + 1) * HD_C]
            outs = []
            for g in range(G_C):
                h = kv * G_C + g
                s = lax.dot_general(q_ref[:, h * HD_C:(h + 1) * HD_C], k_h, NT,
                                    preferred_element_type=f32) + bias_scr[h]
                if first:
                    s = jnp.where(has_prev, s, NEG_INF)
                outs.append(_sink_softmax_pv(s, sink_ref[h], v_h))
            for p in range(G_C // 2):
                tile = jnp.concatenate([outs[2 * p], outs[2 * p + 1]], axis=1)
                col0 = (kv * G_C + 2 * p) * HD_C
                o_ref[:, col0:col0 + LANES] = tile.astype(o_ref.dtype)

    @pl.when(n == 0)
    def _():
        block(True)

    @pl.when(n > 0)
    def _():
        block(False)


def _swa_prompt(q, k, v, rel_bias, sinks):
    nb = SEQ // WINDOW
    bucket = jnp.asarray(_window_bucket_and_mask(WINDOW + np.arange(WINDOW), 2 * WINDOW))
    smem = pl.BlockSpec(memory_space=pltpu.SMEM)
    cur = lambda w: pl.BlockSpec((WINDOW, w), lambda b, n: (b * nb + n, 0))
    prev = lambda w: pl.BlockSpec((WINDOW, w), lambda b, n: (b * nb + jnp.maximum(n - 1, 0), 0))
    kvw = KV_C * HD_C
    return pl.pallas_call(
        _swa_prompt_body,
        grid=(BATCH, nb),
        in_specs=[smem, smem, pl.BlockSpec(bucket.shape, lambda b, n: (0, 0)),
                  cur(H_C * HD_C), prev(kvw), cur(kvw), prev(kvw), cur(kvw)],
        out_specs=cur(H_C * HD_C),
        out_shape=jax.ShapeDtypeStruct((MP, H_C * HD_C), bf16),
        scratch_shapes=[pltpu.VMEM((H_C, WINDOW, 2 * WINDOW), f32)],
        compiler_params=_cparams(("arbitrary", "arbitrary")),
        name="swa_prompt",
    )(rel_bias, sinks, bucket, q, k, k, v, v)


SWA_SB = 4


def _swa_decode_body(rel_ref, sink_ref, bucket_ref, q_ref, kb_ref, vb_ref, kn_ref, vn_ref,
                     o_ref, ko_ref, vo_ref, bias_scr, kn_pad, vn_pad):
    b = pl.program_id(0)

    @pl.when(b == 0)
    def _():
        _fill_bias(rel_ref, bucket_ref[...], bias_scr)
        kn_pad[...] = jnp.zeros(kn_pad.shape, f32)
        vn_pad[...] = jnp.zeros(vn_pad.shape, f32)

    keep = WINDOW - DEC_SEQ
    for sl in range(SWA_SB):
        kn_pad[sl, 0:DEC_SEQ, :] = kn_ref[sl]
        vn_pad[sl, 0:DEC_SEQ, :] = vn_ref[sl]
        keys = jnp.concatenate([kb_ref[sl], kn_pad[sl]], axis=0).astype(bf16)
        vals = jnp.concatenate([vb_ref[sl], vn_pad[sl]], axis=0).astype(bf16)
        q = q_ref[sl]
        for kv in range(KV_C):
            k_h = keys[:, kv * HD_C:(kv + 1) * HD_C]
            v_h = vals[:, kv * HD_C:(kv + 1) * HD_C]
            q_kv = jnp.concatenate([q[:, (kv * G_C + g) * HD_C:(kv * G_C + g + 1) * HD_C] for g in range(G_C)],
                                   axis=0).astype(bf16)
            s_all = lax.dot_general(q_kv, k_h, NT, preferred_element_type=f32)
            outs = []
            for g in range(G_C):
                h = kv * G_C + g
                s = s_all[g * DEC_SEQ:(g + 1) * DEC_SEQ, :] + bias_scr[h]
                outs.append(_sink_softmax_pv(s, sink_ref[h], v_h))
            for p in range(G_C // 2):
                tile = jnp.concatenate([outs[2 * p], outs[2 * p + 1]], axis=1)
                col0 = (kv * G_C + 2 * p) * HD_C
                o_ref[sl, :, col0:col0 + LANES] = tile.astype(o_ref.dtype)
        ko_ref[sl, 0:keep, :] = kb_ref[sl, DEC_SEQ:WINDOW, :]
        ko_ref[sl, keep:WINDOW, :] = kn_ref[sl]
        vo_ref[sl, 0:keep, :] = vb_ref[sl, DEC_SEQ:WINDOW, :]
        vo_ref[sl, keep:WINDOW, :] = vn_ref[sl]


def _swa_decode(q_s, buf_k, buf_v, k_new, v_new, rel_bias, sinks):
    bucket = jnp.asarray(_window_bucket_and_mask(WINDOW + np.arange(DEC_SEQ), 2 * WINDOW))
    smem = pl.BlockSpec(memory_space=pltpu.SMEM)
    kvw = KV_C * HD_C
    seq = lambda r, w: pl.BlockSpec((SWA_SB, r, w), lambda b: (b, 0, 0))
    return pl.pallas_call(
        _swa_decode_body,
        grid=(DEC_BATCH // SWA_SB,),
        in_specs=[smem, smem, pl.BlockSpec(bucket.shape, lambda b: (0, 0)),
                  seq(DEC_SEQ, H_C * HD_C), seq(WINDOW, kvw), seq(WINDOW, kvw), seq(DEC_SEQ, kvw), seq(DEC_SEQ, kvw)],
        out_specs=[seq(DEC_SEQ, H_C * HD_C), seq(WINDOW, kvw), seq(WINDOW, kvw)],
        out_shape=[jax.ShapeDtypeStruct((DEC_BATCH, DEC_SEQ, H_C * HD_C), f32),
                   jax.ShapeDtypeStruct((DEC_BATCH, WINDOW, kvw), f32),
                   jax.ShapeDtypeStruct((DEC_BATCH, WINDOW, kvw), f32)],
        scratch_shapes=[pltpu.VMEM((H_C, DEC_SEQ, 2 * WINDOW), f32), pltpu.VMEM((SWA_SB, WINDOW, kvw), f32),
                        pltpu.VMEM((SWA_SB, WINDOW, kvw), f32)],
        compiler_params=_cparams(("arbitrary",)),
        name="swa_decode",
    )(rel_bias, sinks, bucket, q_s, buf_k, buf_v, k_new, v_new)


def _rope_tables():
    half = ROPE_D // 2
    inv = ROPE_THETA ** (-jnp.arange(half, dtype=f32) / half)
    pos = jnp.concatenate([jnp.tile(jnp.arange(SEQ, dtype=i32), BATCH),
                           jnp.tile(PAST_LEN + jnp.arange(DEC_SEQ, dtype=i32), DEC_BATCH)])
    ang = pos.astype(f32)[:, None] * inv[None, :]
    cos, sin = jnp.cos(ang), jnp.sin(ang)
    return jnp.tile(cos, (1, 4)), jnp.tile(jnp.concatenate([-sin, sin], axis=1), (1, 2))


def _row(v, reps=1):
    return jnp.tile(v.astype(f32), reps).reshape(1, -1)


def _ab_layer(x, j, tabs, caches, page_table, p):
    cos128, sin128 = tabs
    cache_ckv, cache_krope, cache_k, cache_v, cache_logf = caches
    w_in = p["ab_w_in"][j]
    pad = jnp.zeros((D_MODEL, IN_AB_PAD - w_in.shape[1]), f32)
    win = jnp.concatenate([w_in[:, 0:1024], w_in[:, 1088:2624], w_in[:, 1024:1088], w_in[:, 2624:2632], pad],
                          axis=1).astype(bf16)
    wq = p["mla_w_q_up"][j].reshape(Q_LORA, H_A, NOPE_D + ROPE_D)
    wqu = jnp.concatenate([wq[:, :, :NOPE_D].reshape(Q_LORA, -1), wq[:, :, NOPE_D:].reshape(Q_LORA, -1)],
                          axis=1).astype(bf16)
    bf128 = jnp.zeros((1, LANES), f32).at[0, 64:64 + H_B].set(p["fox_b_f"][j])
    gkr128 = jnp.concatenate([p["mla_g_kr"][j], jnp.ones((64,), f32)]).reshape(1, LANES)
    qmla, ckv, krope, kr128, fq, fk, fv, logf = _abproj(
        x, _row(p["mix_g"][2 * j]), win, _row(p["mla_g_q_lat"][j]), wqu, _row(p["mla_g_kv_lat"][j]),
        _row(p["mla_g_qn"][j]), _row(p["mla_g_qr"][j], 2), gkr128, _row(p["fox_g_q"][j]), _row(p["fox_g_k"][j]),
        bf128, cos128, sin128)
    wuk = p["mla_w_uk"][j].astype(bf16)
    wuv = p["mla_w_uv"][j].astype(bf16)
    gkn = _row(p["mla_g_kn"][j])
    proj = (qmla, ckv, krope, kr128, fq, fk, fv, logf)
    o_p = _ab_prompt_attn(proj, wuk, wuv, gkn)
    o_s = _ab_sample_attn(proj, caches, page_table, wuk, wuv, gkn, j)
    x = _outproj(x, jnp.concatenate([o_p, o_s], axis=0), p["ab_w_out"][j].astype(bf16))
    seqs = lambda a: a[MP:].reshape(DEC_BATCH, DEC_SEQ, -1)
    st_p = (ckv[:MP].reshape(BATCH, SEQ, KV_LORA), krope[:MP].reshape(BATCH, SEQ, ROPE_D),
            fk[:MP].reshape(BATCH, SEQ, KV_B, HD_B), fv[:MP].reshape(BATCH, SEQ, KV_B, HD_B),
            logf[:MP].reshape(BATCH, SEQ, H_B))
    st_s = (seqs(ckv), seqs(krope), seqs(fk).reshape(DEC_BATCH, DEC_SEQ, KV_B, HD_B),
            seqs(fv).reshape(DEC_BATCH, DEC_SEQ, KV_B, HD_B), seqs(logf))
    return x, st_p, st_s


def _ab_prompt_attn(proj, wuk, wuv, gkn):
    qmla, ckv, krope, kr128, fq, fk, fv, logf = proj
    kmla, va = _kvup(ckv, kr128, wuk, wuv, gkn)
    o_a = _flash_prompt(qmla, kmla, va, H_A, 2, 1, QH, V_D)
    lf_p = logf[:MP].reshape(BATCH, SEQ, H_B)
    f_t = _cumsum_prompt(jnp.swapaxes(lf_p, 1, 2))
    f_tok = jnp.swapaxes(f_t, 1, 2).reshape(MP, H_B)
    o_b = _flash_prompt(fq, fk, fv, KV_B, KV_B, G_B, HD_B, HD_B, fq=f_tok, fk=f_t)
    return jnp.concatenate([o_a, o_b], axis=1)


def _ab_sample_attn(proj, caches, page_table, wuk, wuv, gkn, j):
    qmla, ckv, krope, kr128, fq, fk, fv, logf = proj
    cache_ckv, cache_krope_t, cache_k, cache_v, cache_logf_t = caches
    seqs = lambda a: a[MP:].reshape(DEC_BATCH, DEC_SEQ, -1)
    lf_new_t = jnp.swapaxes(seqs(logf), 1, 2).reshape(DEC_BATCH * H_B, DEC_SEQ)
    lf_new_t = jnp.pad(lf_new_t, ((0, 0), (0, CS_CHUNK - DEC_SEQ)))
    fk_past, fn_new = _lf_sample(page_table, cache_logf_t, lf_new_t, j)
    fk_past = fk_past.reshape(DEC_BATCH, H_B, PAST_LEN)
    fn_new = fn_new.reshape(DEC_BATCH, H_B, LANES)
    fq_col = fn_new[:, :, :DEC_SEQ].reshape(DEC_BATCH, H_B * DEC_SEQ, 1)
    os_a = _mla_decode(page_table, cache_ckv, cache_krope_t, seqs(qmla).astype(f32), seqs(ckv), seqs(krope),
                       wuk.T, wuv, gkn, j)
    os_b = _fox_decode(page_table, cache_k, cache_v, seqs(fq).astype(f32), seqs(fk), seqs(fv),
                       fq_col, fk_past, fn_new, j)
    return jnp.concatenate([os_a, os_b], axis=2).reshape(MS, D_MODEL).astype(bf16)


def _c_layer(x, j, state_k, state_v, p):
    kvw = KV_C * HD_C
    q, k, v = _cproj(x, _row(p["mix_g"][2 * j + 1]), p["swa_w_in"][j].astype(bf16),
                     _row(p["swa_g_q"][j], 2), _row(p["swa_g_k"][j], 2))
    rel = p["rel_bias"].astype(f32)
    sinks = p["swa_sinks"][j].astype(f32)
    o_p = _swa_prompt(q, k, v, rel, sinks)
    seqs = lambda a: a[MP:].reshape(DEC_BATCH, DEC_SEQ, -1)
    o_s, k_out, v_out = _swa_decode(seqs(q).astype(f32), state_k[j].reshape(DEC_BATCH, WINDOW, kvw),
                                    state_v[j].reshape(DEC_BATCH, WINDOW, kvw), seqs(k), seqs(v), rel, sinks)
    o = jnp.concatenate([o_p, o_s.reshape(MS, D_MODEL).astype(bf16)], axis=0)
    x = _outproj(x, o, p["swa_w_out"][j].astype(bf16))
    tail = lambda a: a[:MP].reshape(BATCH, SEQ, KV_C, HD_C)[:, SEQ - WINDOW:]
    st_p = (tail(k), tail(v))
    st_s = (k_out.reshape(DEC_BATCH, WINDOW, KV_C, HD_C), v_out.reshape(DEC_BATCH, WINDOW, KV_C, HD_C))
    return x, st_p, st_s


def kernel(x_prompt, x_sample, cache_mla_ckv, cache_mla_krope, cache_fox_k, cache_fox_v, cache_fox_logf, state_swa_k, state_swa_v, page_table, ffn1_g, ffn1_w_gate, ffn1_w_up, ffn1_w_down, mix_g, ffn2_g, ffn2_w_gate, ffn2_w_up, ffn2_w_down, ab_w_in, mla_g_q_lat, mla_w_q_up, mla_g_kv_lat, mla_w_uk, mla_w_uv, mla_g_qn, mla_g_qr, mla_g_kn, mla_g_kr, fox_g_q, fox_g_k, fox_b_f, ab_w_out, swa_w_in, swa_g_q, swa_g_k, swa_sinks, swa_w_out, rel_bias):
    assert WINDOW == PAGE_SIZE == LANES and min(WINDOW, PAST_LEN) == WINDOW
    p = dict(mix_g=mix_g, ab_w_in=ab_w_in, mla_g_q_lat=mla_g_q_lat, mla_w_q_up=mla_w_q_up,
             mla_g_kv_lat=mla_g_kv_lat, mla_w_uk=mla_w_uk, mla_w_uv=mla_w_uv, mla_g_qn=mla_g_qn,
             mla_g_qr=mla_g_qr, mla_g_kn=mla_g_kn, mla_g_kr=mla_g_kr, fox_g_q=fox_g_q, fox_g_k=fox_g_k,
             fox_b_f=fox_b_f, ab_w_out=ab_w_out, swa_w_in=swa_w_in, swa_g_q=swa_g_q, swa_g_k=swa_g_k,
             swa_sinks=swa_sinks, swa_w_out=swa_w_out, rel_bias=rel_bias)
    n_pool = cache_fox_k.shape[1]
    caches = (cache_mla_ckv, jnp.swapaxes(cache_mla_krope, 2, 3),
              cache_fox_k.reshape(-1, n_pool, PAGE_SIZE * KV_B, HD_B),
              cache_fox_v.reshape(-1, n_pool, PAGE_SIZE * KV_B, HD_B), jnp.swapaxes(cache_fox_logf, 2, 3))
    tabs = _rope_tables()
    x = jnp.concatenate([x_prompt.reshape(MP, D_MODEL), x_sample.reshape(MS, D_MODEL)], axis=0)
    ab_p, ab_s, c_p, c_s = [], [], [], []
    for l in range(DEPTH):
        x = _ffn(x, ffn1_g[l], ffn1_w_gate, ffn1_w_up, ffn1_w_down, l)
        j = l // 2
        if l % 2 == 0:
            x, st_p, st_s = _ab_layer(x, j, tabs, caches, page_table, p)
            ab_p.append(st_p)
            ab_s.append(st_s)
        else:
            x, st_p, st_s = _c_layer(x, j, state_swa_k, state_swa_v, p)
            c_p.append(st_p)
            c_s.append(st_s)
        x = _ffn(x, ffn2_g[l], ffn2_w_gate, ffn2_w_up, ffn2_w_down, l)

    def stack(states, i):
        return jnp.stack([st[i] for st in states], axis=0)

    return (x[:MP].reshape(BATCH, SEQ, D_MODEL), x[MP:].reshape(DEC_BATCH, DEC_SEQ, D_MODEL),
            stack(ab_p, 0), stack(ab_p, 1), stack(ab_p, 2), stack(ab_p, 3), stack(ab_p, 4),
            stack(c_p, 0), stack(c_p, 1),
            stack(ab_s, 0), stack(ab_s, 1), stack(ab_s, 2), stack(ab_s, 3), stack(ab_s, 4),
            stack(c_s, 0), stack(c_s, 1))
```

```python
import functools
import math

import numpy as np
import jax
import jax.numpy as jnp
from jax import lax
from jax.experimental import pallas as pl
from jax.experimental.pallas import tpu as pltpu

f32, bf16, i32 = jnp.float32, jnp.bfloat16, jnp.int32

D_MODEL = 2048
BATCH = 4
SEQ = 2048
DEPTH = 4
DEC_BATCH = 128
DEC_SEQ = 8
PAST_LEN = 8192
PAGE_SIZE = 128
N_PAGES = PAST_LEN // PAGE_SIZE
H_A = 8
Q_LORA = 512
KV_LORA = 512
NOPE_D = 128
ROPE_D = 64
V_D = 128
ROPE_THETA = 10000.0
H_B = 8
KV_B = 2
HD_B = 128
H_C = 32
KV_C = 4
HD_C = 64
WINDOW = 128
NUM_BUCKETS = 32
MAX_DISTANCE = 128
D_FF = 4096
FFN_RES = 0.5
EPS = 1e-6
NEG_INF = -1e30
MLA_SCALE = (NOPE_D + ROPE_D) ** -0.5
FOX_SCALE = HD_B ** -0.5
C_SCALE = HD_C ** -0.5
G_B = H_B // KV_B
G_C = H_C // KV_C
MAX_EXACT = NUM_BUCKETS // 2

MP = BATCH * SEQ
MS = DEC_BATCH * DEC_SEQ
MT = MP + MS
IN_AB_PAD = 2688
QH = 256

LANES = 128
MXU_DIM = 256
VMEM_LIMIT = 56 * 1024 * 1024

NT = (((1,), (1,)), ((), ()))


def _cparams(sem, vmem_limit=VMEM_LIMIT):
    return pltpu.CompilerParams(dimension_semantics=sem, vmem_limit_bytes=vmem_limit)


def _rms(x, g, eps=EPS):
    return (x * lax.rsqrt(jnp.mean(x * x, axis=-1, keepdims=True) + eps)) * g


def _split3(x):
    hi = x.astype(bf16)
    r1 = x - hi.astype(f32)
    mid = r1.astype(bf16)
    lo = (r1 - mid.astype(f32)).astype(bf16)
    return hi, mid, lo


FFN_TM = 1024
FFN_TF = 256
FFN_VMEM_LIMIT = (4 * FFN_TM * D_MODEL * 4 + FFN_TM * D_MODEL * 2 + 6 * D_MODEL * FFN_TF * 4
                  + 12 * FFN_TM * FFN_TF * 4)


def _ffn_body(x_ref, g_ref, wg_ref, wu_ref, wd_ref, o_ref, n_ref):
    @pl.when(pl.program_id(1) == 0)
    def _():
        x = x_ref[...]
        n_ref[...] = _rms(x, g_ref[...]).astype(bf16)
        o_ref[...] = x

    n = n_ref[...]
    gate = jnp.dot(n, wg_ref[...].astype(bf16), preferred_element_type=f32)
    up = jnp.dot(n, wu_ref[...].astype(bf16), preferred_element_type=f32)
    h = (gate / (1.0 + jnp.exp(-gate))) * up * FFN_RES
    o_ref[...] += jnp.dot(h.astype(bf16), wd_ref[...].astype(bf16), preferred_element_type=f32)


def _ffn(x, g, wg, wu, wd, l):
    m = x.shape[0]
    return pl.pallas_call(
        _ffn_body,
        grid=(m // FFN_TM, D_FF // FFN_TF),
        in_specs=[
            pl.BlockSpec((FFN_TM, D_MODEL), lambda i, f: (i, 0)),
            pl.BlockSpec((1, D_MODEL), lambda i, f: (0, 0)),
            pl.BlockSpec((None, D_MODEL, FFN_TF), lambda i, f: (l, 0, f)),
            pl.BlockSpec((None, D_MODEL, FFN_TF), lambda i, f: (l, 0, f)),
            pl.BlockSpec((None, FFN_TF, D_MODEL), lambda i, f: (l, f, 0)),
        ],
        out_specs=pl.BlockSpec((FFN_TM, D_MODEL), lambda i, f: (i, 0)),
        out_shape=jax.ShapeDtypeStruct((m, D_MODEL), f32),
        scratch_shapes=[pltpu.VMEM((FFN_TM, D_MODEL), bf16)],
        compiler_params=_cparams(("parallel", "arbitrary"), FFN_VMEM_LIMIT),
        name="ffn_half",
    )(x, g.reshape(1, D_MODEL), wg, wu, wd)


OUT_TM = 512


def _outproj_body(x_ref, o_ref, w_ref, y_ref):
    y_ref[...] = x_ref[...] + jnp.dot(o_ref[...], w_ref[...], preferred_element_type=f32)


def _outproj(x, o, w):
    m = x.shape[0]
    return pl.pallas_call(
        _outproj_body,
        grid=(m // OUT_TM,),
        in_specs=[
            pl.BlockSpec((OUT_TM, D_MODEL), lambda i: (i, 0)),
            pl.BlockSpec((OUT_TM, D_MODEL), lambda i: (i, 0)),
            pl.BlockSpec((D_MODEL, D_MODEL), lambda i: (0, 0)),
        ],
        out_specs=pl.BlockSpec((OUT_TM, D_MODEL), lambda i: (i, 0)),
        out_shape=jax.ShapeDtypeStruct((m, D_MODEL), f32),
        compiler_params=_cparams(("parallel",)),
        name="out_proj",
    )(x, o, w)


AB_TM = 256


def _half_masks(shape):
    lane = lax.broadcasted_iota(i32, shape, 1)
    return lane < 64, (lane % 64) < 32


def _norm_rope_tile(t, g128, cosv, sinv, lo, first):
    ss = t * t
    s_lo = jnp.sum(jnp.where(lo, ss, 0.0), axis=-1, keepdims=True)
    s_hi = jnp.sum(jnp.where(lo, 0.0, ss), axis=-1, keepdims=True)
    r = jnp.where(lo, lax.rsqrt(s_lo / 64.0 + EPS), lax.rsqrt(s_hi / 64.0 + EPS))
    y = (t * r) * g128
    other = jnp.where(first, pltpu.roll(y, 96, 1), pltpu.roll(y, 32, 1))
    return y * cosv + other * sinv


def _abproj_body(x_ref, mg_ref, win_ref, gql_ref, wqu_ref, gkv_ref, gqn_ref, gqr_ref, gkr_ref,
                 gfq_ref, gfk_ref, bf_ref, cos_ref, sin_ref,
                 qmla_ref, ckv_ref, krope_ref, kr128_ref, fq_ref, fk_ref, fv_ref, logf_ref):
    n = _rms(x_ref[...], mg_ref[...]).astype(bf16)
    y = jnp.dot(n, win_ref[...], preferred_element_type=f32)
    cosv, sinv = cos_ref[...], sin_ref[...]
    lo, first = _half_masks(cosv.shape)

    ql = _rms(y[:, 0:Q_LORA], gql_ref[...]).astype(bf16)
    q = jnp.dot(ql, wqu_ref[...], preferred_element_type=f32)
    for h in range(H_A):
        t = _rms(q[:, h * NOPE_D:(h + 1) * NOPE_D], gqn_ref[...]) * MLA_SCALE
        qmla_ref[:, h * QH:h * QH + NOPE_D] = t.astype(bf16)
    base = H_A * NOPE_D
    for p in range(H_A // 2):
        t = _norm_rope_tile(q[:, base + p * LANES:base + (p + 1) * LANES], gqr_ref[...],
                            cosv, sinv, lo, first) * MLA_SCALE
        even = jnp.where(lo, t, 0.0)
        odd = jnp.where(lo, pltpu.roll(t, 64, 1), 0.0)
        qmla_ref[:, (2 * p) * QH + NOPE_D:(2 * p + 1) * QH] = even.astype(bf16)
        qmla_ref[:, (2 * p + 1) * QH + NOPE_D:(2 * p + 2) * QH] = odd.astype(bf16)

    ckv_ref[...] = _rms(y[:, 512:1024], gkv_ref[...])

    for h in range(H_B):
        t = _rms(y[:, 1024 + h * HD_B:1024 + (h + 1) * HD_B], gfq_ref[...]) * FOX_SCALE
        fq_ref[:, h * HD_B:(h + 1) * HD_B] = t.astype(bf16)
    for k in range(KV_B):
        fk_ref[:, k * HD_B:(k + 1) * HD_B] = _rms(y[:, 2048 + k * HD_B:2048 + (k + 1) * HD_B], gfk_ref[...])
    fv_ref[...] = y[:, 2304:2560]

    t = y[:, 2560:2688]
    kr = _norm_rope_tile(t, gkr_ref[...], cosv, sinv, lo, first)
    krope_ref[...] = kr[:, 0:ROPE_D]
    kr128_ref[...] = jnp.where(lo, kr, 0.0).astype(bf16)
    z = t + bf_ref[...]
    ls = jnp.minimum(z, 0.0) - jnp.log(1.0 + jnp.exp(-jnp.abs(z)))
    logf_ref[...] = ls[:, 64:64 + H_B]


def _abproj(x, mg, win, gql, wqu, gkv, gqn, gqr128, gkr128, gfq, gfk, bf128, cos128, sin128):
    m = x.shape[0]
    row = lambda w: pl.BlockSpec((AB_TM, w), lambda i: (i, 0))
    full = lambda a: pl.BlockSpec(a.shape, lambda i: (0,) * a.ndim)
    ins = [x, mg, win, gql, wqu, gkv, gqn, gqr128, gkr128, gfq, gfk, bf128, cos128, sin128]
    in_specs = [row(D_MODEL)] + [full(a) for a in ins[1:12]] + [row(LANES), row(LANES)]
    outs = [(H_A * QH, bf16), (KV_LORA, f32), (ROPE_D, f32), (LANES, bf16), (H_B * HD_B, bf16),
            (KV_B * HD_B, f32), (KV_B * HD_B, f32), (H_B, f32)]
    return pl.pallas_call(
        _abproj_body,
        grid=(m // AB_TM,),
        in_specs=in_specs,
        out_specs=[row(w) for w, _ in outs],
        out_shape=[jax.ShapeDtypeStruct((m, w), dt) for w, dt in outs],
        compiler_params=_cparams(("parallel",)),
        name="ab_proj",
    )(*ins)


KV_TM = 512


def _kvup_body(ckv_ref, kr128_ref, wuk_ref, wuv_ref, gkn_ref, kmla_ref, va_ref):
    c = ckv_ref[...].astype(bf16)
    kk = jnp.dot(c, wuk_ref[...], preferred_element_type=f32)
    kr = kr128_ref[...]
    for h in range(H_A):
        kmla_ref[:, h * QH:h * QH + NOPE_D] = _rms(kk[:, h * NOPE_D:(h + 1) * NOPE_D], gkn_ref[...]).astype(bf16)
        kmla_ref[:, h * QH + NOPE_D:(h + 1) * QH] = kr
    va_ref[...] = jnp.dot(c, wuv_ref[...], preferred_element_type=f32).astype(bf16)


def _kvup(ckv, kr128, wuk, wuv, gkn):
    m = MP
    row = lambda w: pl.BlockSpec((KV_TM, w), lambda i: (i, 0))
    full = lambda a: pl.BlockSpec(a.shape, lambda i: (0,) * a.ndim)
    return pl.pallas_call(
        _kvup_body,
        grid=(m // KV_TM,),
        in_specs=[row(KV_LORA), row(LANES), full(wuk), full(wuv), full(gkn)],
        out_specs=[row(H_A * QH), row(H_A * V_D)],
        out_shape=[jax.ShapeDtypeStruct((m, H_A * QH), bf16), jax.ShapeDtypeStruct((m, H_A * V_D), bf16)],
        compiler_params=_cparams(("parallel",)),
        name="mla_kv_up",
    )(ckv, kr128, wuk, wuv, gkn)


CS_CHUNK = MXU_DIM


def _tri_ones(n):
    r = lax.broadcasted_iota(i32, (n, n), 0)
    c = lax.broadcasted_iota(i32, (n, n), 1)
    return jnp.where(r <= c, 1.0, 0.0).astype(bf16)


def _cumsum_chunks(parts_of, n_chunks, write, carry):
    tri = _tri_ones(CS_CHUNK)
    for ch in range(n_chunks):
        cs = carry
        for p in parts_of(ch):
            cs = cs + jnp.dot(p, tri, preferred_element_type=f32)
        write(ch, cs)
        carry = cs[:, CS_CHUNK - 1:CS_CHUNK]
    return carry


def _cumsum_body(x_ref, o_ref):
    def parts_of(ch):
        return _split3(x_ref[:, ch * CS_CHUNK:(ch + 1) * CS_CHUNK])

    def write(ch, cs):
        o_ref[:, ch * CS_CHUNK:(ch + 1) * CS_CHUNK] = cs

    _cumsum_chunks(parts_of, SEQ // CS_CHUNK, write, jnp.zeros((H_B, 1), f32))


def _cumsum_prompt(lft):
    return pl.pallas_call(
        _cumsum_body,
        grid=(BATCH,),
        in_specs=[pl.BlockSpec((None, H_B, SEQ), lambda b: (b, 0, 0))],
        out_specs=pl.BlockSpec((None, H_B, SEQ), lambda b: (b, 0, 0)),
        out_shape=jax.ShapeDtypeStruct((BATCH, H_B, SEQ), f32),
        compiler_params=_cparams(("parallel",)),
        name="fox_cumsum_prompt",
    )(lft)


FL_T = 512


def _flash_body(*refs, NH, G, DQ, DV, has_bias, nk):
    if has_bias:
        q_ref, k_ref, v_ref, fq_ref, fk_ref, o_ref, m_scr, acc_scr = refs
    else:
        q_ref, k_ref, v_ref, o_ref, m_scr, acc_scr = refs
    assert DV == LANES
    i = pl.program_id(2)
    j = pl.program_id(3)

    @pl.when(j == 0)
    def _():
        m_scr[...] = jnp.full(m_scr.shape, NEG_INF, f32)
        acc_scr[...] = jnp.zeros(acc_scr.shape, f32)

    def tile(diagonal):
        if diagonal:
            mask = (lax.broadcasted_iota(i32, (FL_T, FL_T), 1) <= lax.broadcasted_iota(i32, (FL_T, FL_T), 0))
        ones = jnp.ones((FL_T, LANES), bf16)
        for kv in range(NH):
            k = k_ref[:, kv * DQ:(kv + 1) * DQ].astype(bf16)
            v1 = jnp.concatenate([v_ref[:, kv * DV:(kv + 1) * DV].astype(bf16), ones], axis=1)
            for g in range(G):
                h = kv * G + g
                s = lax.dot_general(q_ref[:, h * DQ:(h + 1) * DQ], k, NT, preferred_element_type=f32)
                if has_bias:
                    s = s + (fq_ref[:, h:h + 1] - fk_ref[h:h + 1, :])
                if diagonal:
                    s = jnp.where(mask, s, NEG_INF)
                m_prev = m_scr[h]
                m_new = jnp.maximum(m_prev, jnp.max(s, axis=-1, keepdims=True))
                alpha = jnp.exp(m_prev - m_new)
                p = jnp.exp(s - jnp.concatenate([m_new] * (FL_T // LANES), axis=1))
                acc_scr[h] = (jnp.concatenate([alpha, alpha], axis=1) * acc_scr[h]
                              + jnp.dot(p.astype(bf16), v1, preferred_element_type=f32))
                m_scr[h] = m_new

    @pl.when(j < i)
    def _():
        tile(False)

    @pl.when(j == i)
    def _():
        tile(True)

    @pl.when(j == nk - 1)
    def _():
        for h in range(NH * G):
            acc = acc_scr[h]
            o_ref[:, h * DV:(h + 1) * DV] = (acc[:, 0:DV] / acc[:, DV:2 * DV]).astype(o_ref.dtype)


def _flash_prompt(q, k, v, n_kv, NH, G, DQ, DV, fq=None, fk=None):
    nq = SEQ // FL_T
    has_bias = fq is not None
    qmap = lambda b, h, i, j: (b * nq + i, h)
    kmap = lambda b, h, i, j: (b * nq + jnp.minimum(i, j), h)
    in_specs = [pl.BlockSpec((FL_T, NH * G * DQ), qmap), pl.BlockSpec((FL_T, NH * DQ), kmap),
                pl.BlockSpec((FL_T, NH * DV), kmap)]
    ins = [q, k, v]
    if has_bias:
        assert NH * G == H_B
        in_specs += [pl.BlockSpec((FL_T, H_B), lambda b, h, i, j: (b * nq + i, 0)),
                     pl.BlockSpec((None, H_B, FL_T), lambda b, h, i, j: (b, 0, jnp.minimum(i, j)))]
        ins += [fq, fk]
    nh = NH * G
    return pl.pallas_call(
        functools.partial(_flash_body, NH=NH, G=G, DQ=DQ, DV=DV, has_bias=has_bias, nk=nq),
        grid=(BATCH, n_kv // NH, nq, nq),
        in_specs=in_specs,
        out_specs=pl.BlockSpec((FL_T, nh * DV), qmap),
        out_shape=jax.ShapeDtypeStruct((MP, n_kv * G * DV), bf16),
        scratch_shapes=[pltpu.VMEM((nh, FL_T, LANES), f32), pltpu.VMEM((nh, FL_T, 2 * DV), f32)],
        compiler_params=_cparams(("parallel", "parallel", "parallel", "arbitrary")),
        name="flash_prompt_bias" if has_bias else "flash_prompt",
    )(*ins)


def _page_copies(pt_ref, layer, b, c, slot, pages, specs):
    out = []
    for g in range(pages):
        page = pt_ref[b, c * pages + g]
        for hbm_ref, buf_ref, sem_ref, rows in specs:
            if rows:
                dst = buf_ref.at[slot, pl.ds(g * rows, rows), :]
            else:
                dst = buf_ref.at[slot, :, pl.ds(g * PAGE_SIZE, PAGE_SIZE)]
            out.append(pltpu.make_async_copy(hbm_ref.at[layer, page], dst, sem_ref.at[slot]))
    return out


def _gather_wait(pt_ref, layer, n_chunks, pages, specs):
    b, c = pl.program_id(0), pl.program_id(1)
    n = b * n_chunks + c
    slot = n % 2

    @pl.when(n == 0)
    def _():
        for cp in _page_copies(pt_ref, layer, 0, 0, 0, pages, specs):
            cp.start()

    for cp in _page_copies(pt_ref, layer, b, c, slot, pages, specs):
        cp.wait()
    return slot


def _gather_prefetch(pt_ref, layer, n_chunks, pages, specs):
    n = pl.program_id(0) * n_chunks + pl.program_id(1)
    nxt = (n + 1) % (pl.num_programs(0) * n_chunks)
    for cp in _page_copies(pt_ref, layer, nxt // n_chunks, nxt % n_chunks, (n + 1) % 2, pages, specs):
        cp.start()


def _gather_drain(pt_ref, layer, n_chunks, pages, specs):
    n = pl.program_id(0) * n_chunks + pl.program_id(1)

    @pl.when(n == pl.num_programs(0) * n_chunks - 1)
    def _():
        for cp in _page_copies(pt_ref, layer, 0, 0, (n + 1) % 2, pages, specs):
            cp.wait()


def _softmax_update(s, v, m_ref, l_ref, acc_ref):
    m_prev = m_ref[...]
    m_new = jnp.maximum(m_prev, jnp.max(s, axis=-1, keepdims=True))
    alpha = jnp.exp(m_prev - m_new)
    p = jnp.exp(s - m_new)
    l_ref[...] = alpha * l_ref[...] + jnp.sum(p, axis=-1, keepdims=True)
    acc_ref[...] = alpha * acc_ref[...] + jnp.dot(p.astype(bf16), v, preferred_element_type=f32)
    m_ref[...] = m_new


def _new_key_mask(rows):
    t = lax.broadcasted_iota(i32, (rows, LANES), 0) % DEC_SEQ
    col = lax.broadcasted_iota(i32, (rows, LANES), 1)
    return (col < DEC_SEQ) & (col <= t)


LF_ROWS = PAST_LEN + CS_CHUNK
LF_SB = 16


def _lf_copies(pt_ref, cache_ref, buf, sem, layer, step, slot, start):
    def one_seq(sl, carry):
        for p in range(N_PAGES):
            cp = pltpu.make_async_copy(cache_ref.at[layer, pt_ref[step * LF_SB + sl, p]],
                                       buf.at[slot, pl.ds(sl * H_B, H_B), pl.ds(p * PAGE_SIZE, PAGE_SIZE)],
                                       sem.at[slot])
            if start:
                cp.start()
            else:
                cp.wait()
        return carry
    lax.fori_loop(0, LF_SB, one_seq, 0)


def _lf_sample_body(pt_ref, cache_ref, lfn_ref, fk_ref, fn_ref, buf, sem, *, layer):
    n = pl.program_id(0)
    slot = n % 2

    @pl.when(n == 0)
    def _():
        _lf_copies(pt_ref, cache_ref, buf, sem, layer, 0, 0, True)

    @pl.when(n + 1 < pl.num_programs(0))
    def _():
        _lf_copies(pt_ref, cache_ref, buf, sem, layer, n + 1, (n + 1) % 2, True)

    _lf_copies(pt_ref, cache_ref, buf, sem, layer, n, slot, False)
    buf[slot, :, PAST_LEN:LF_ROWS] = lfn_ref[...]

    def parts_of(ch):
        return _split3(buf[slot, :, ch * CS_CHUNK:(ch + 1) * CS_CHUNK])

    def write(ch, cs):
        if ch < PAST_LEN // CS_CHUNK:
            fk_ref[:, ch * CS_CHUNK:(ch + 1) * CS_CHUNK] = cs
        else:
            fn_ref[...] = cs[:, 0:LANES]

    _cumsum_chunks(parts_of, LF_ROWS // CS_CHUNK, write, jnp.zeros((LF_SB * H_B, 1), f32))


def _lf_sample(page_table, cache_logf_t, lf_new_t, layer):
    rows = LF_SB * H_B
    grid_spec = pltpu.PrefetchScalarGridSpec(
        num_scalar_prefetch=1,
        grid=(DEC_BATCH // LF_SB,),
        in_specs=[pl.BlockSpec(memory_space=pl.ANY),
                  pl.BlockSpec((rows, CS_CHUNK), lambda n, pt: (n, 0))],
        out_specs=[pl.BlockSpec((rows, PAST_LEN), lambda n, pt: (n, 0)),
                   pl.BlockSpec((rows, LANES), lambda n, pt: (n, 0))],
        scratch_shapes=[pltpu.VMEM((2, rows, LF_ROWS), f32), pltpu.SemaphoreType.DMA((2,))],
    )
    return pl.pallas_call(
        functools.partial(_lf_sample_body, layer=layer),
        grid_spec=grid_spec,
        out_shape=[jax.ShapeDtypeStruct((DEC_BATCH * H_B, PAST_LEN), f32),
                   jax.ShapeDtypeStruct((DEC_BATCH * H_B, LANES), f32)],
        compiler_params=_cparams(("arbitrary",)),
        name="fox_logf_sample",
    )(page_table, cache_logf_t, lf_new_t)


MLA_PAGES = 64
MLA_KEYS = MLA_PAGES * PAGE_SIZE
MLA_CHUNKS = N_PAGES // MLA_PAGES
DEC_SUB = MXU_DIM
QROWS = H_A * DEC_SEQ


def _mla_scores(a_ref, c_bf, rope):
    r = lax.dot_general(a_ref[...], c_bf, NT, preferred_element_type=f32)
    out = []
    for h in range(H_A):
        kk = r[h * NOPE_D:(h + 1) * NOPE_D, :]
        inv = lax.rsqrt(jnp.sum(kk * kk, axis=0, keepdims=True) / NOPE_D + EPS)
        lo_, hi_ = H_A * NOPE_D + h * DEC_SEQ, H_A * NOPE_D + (h + 1) * DEC_SEQ
        out.append(r[lo_:hi_, :] * inv)
    return jnp.concatenate(out, axis=0) + rope


def _mla_decode_body(pt_ref, ckv_hbm, kr_hbm, q_ref, cn_ref, krn_ref, wukt_ref, wuv_ref, gkn_ref, o_ref,
                     cbuf, krbuf, csem, krsem, a_scr, qr_scr, s_scr, cbf_scr, cn_pad, krn_pad, m_scr, l_scr, acc_scr,
                     *, layer):
    b, c = pl.program_id(0), pl.program_id(1)
    nk = H_A * NOPE_D

    @pl.when((b == 0) & (c == 0))
    def _():
        a_scr[0:nk, :] = wukt_ref[...]
        cn_pad[...] = jnp.zeros(cn_pad.shape, f32)
        krn_pad[...] = jnp.zeros(krn_pad.shape, f32)

    gather = (pt_ref, layer, MLA_CHUNKS, MLA_PAGES, [(ckv_hbm, cbuf, csem, PAGE_SIZE), (kr_hbm, krbuf, krsem, 0)])
    _gather_prefetch(*gather)
    slot = _gather_wait(*gather)

    @pl.when(c == 0)
    def _():
        q = q_ref[...]
        wq, qr_rows = [], []
        for h in range(H_A):
            qg = (q[:, h * QH:h * QH + NOPE_D] * gkn_ref[...]).astype(bf16)
            wq.append(jnp.dot(qg, wukt_ref[h * NOPE_D:(h + 1) * NOPE_D, :], preferred_element_type=f32))
            qr_rows.append(q[:, h * QH + NOPE_D:h * QH + NOPE_D + ROPE_D])
        a_scr[nk:nk + QROWS, :] = jnp.concatenate(wq, axis=0).astype(bf16)
        qr_scr[...] = jnp.concatenate(qr_rows, axis=0).astype(bf16)
        m_scr[...] = jnp.full(m_scr.shape, NEG_INF, f32)
        l_scr[...] = jnp.zeros(l_scr.shape, f32)
        acc_scr[...] = jnp.zeros(acc_scr.shape, f32)

    qr = qr_scr[...]
    for sub in range(MLA_KEYS // DEC_SUB):
        keys = pl.ds(sub * DEC_SUB, DEC_SUB)
        rope = jnp.dot(qr, krbuf[slot, :, keys].astype(bf16), preferred_element_type=f32)
        c_bf = cbuf[slot, keys, :].astype(bf16)
        cbf_scr[sub * DEC_SUB:(sub + 1) * DEC_SUB, :] = c_bf
        s_scr[:, sub * DEC_SUB:(sub + 1) * DEC_SUB] = _mla_scores(a_scr, c_bf, rope)
    _softmax_update(s_scr[...], cbf_scr[...], m_scr, l_scr, acc_scr)

    @pl.when(c == MLA_CHUNKS - 1)
    def _():
        cn_pad[0:DEC_SEQ, :] = cn_ref[...]
        krn_pad[0:DEC_SEQ, :] = krn_ref[...]
        cn = cn_pad[...].astype(bf16)
        rope = lax.dot_general(qr, krn_pad[...].astype(bf16), NT, preferred_element_type=f32)
        s = _mla_scores(a_scr, cn, rope)
        s = jnp.where(_new_key_mask(QROWS), s, NEG_INF)
        _softmax_update(s, cn, m_scr, l_scr, acc_scr)
        o_lat = (acc_scr[...] / l_scr[...]).astype(bf16)
        for h in range(H_A):
            o_ref[:, h * V_D:(h + 1) * V_D] = jnp.dot(
                o_lat[h * DEC_SEQ:(h + 1) * DEC_SEQ, :], wuv_ref[:, h * V_D:(h + 1) * V_D],
                preferred_element_type=f32).astype(o_ref.dtype)

    _gather_drain(*gather)


def _mla_decode(page_table, cache_ckv, cache_krope, q_s, c_new, kr_new, wukt, wuv, gkn, layer):
    seq = lambda w: pl.BlockSpec((None, DEC_SEQ, w), lambda b, c, pt: (b, 0, 0))
    full = lambda a: pl.BlockSpec(a.shape, lambda b, c, pt: (0,) * a.ndim)
    grid_spec = pltpu.PrefetchScalarGridSpec(
        num_scalar_prefetch=1,
        grid=(DEC_BATCH, MLA_CHUNKS),
        in_specs=[pl.BlockSpec(memory_space=pl.ANY), pl.BlockSpec(memory_space=pl.ANY),
                  seq(H_A * QH), seq(KV_LORA), seq(ROPE_D), full(wukt), full(wuv), full(gkn)],
        out_specs=seq(H_A * V_D),
        scratch_shapes=[
            pltpu.VMEM((2, MLA_KEYS, KV_LORA), f32), pltpu.VMEM((2, ROPE_D, MLA_KEYS), f32),
            pltpu.SemaphoreType.DMA((2,)), pltpu.SemaphoreType.DMA((2,)),
            pltpu.VMEM((H_A * NOPE_D + QROWS, KV_LORA), bf16), pltpu.VMEM((QROWS, ROPE_D), bf16),
            pltpu.VMEM((QROWS, MLA_KEYS), f32), pltpu.VMEM((MLA_KEYS, KV_LORA), bf16),
            pltpu.VMEM((LANES, KV_LORA), f32), pltpu.VMEM((LANES, ROPE_D), f32),
            pltpu.VMEM((QROWS, 1), f32), pltpu.VMEM((QROWS, 1), f32), pltpu.VMEM((QROWS, KV_LORA), f32),
        ],
    )
    return pl.pallas_call(
        functools.partial(_mla_decode_body, layer=layer),
        grid_spec=grid_spec,
        out_shape=jax.ShapeDtypeStruct((DEC_BATCH, DEC_SEQ, H_A * V_D), f32),
        compiler_params=_cparams(("arbitrary", "arbitrary")),
        name="mla_decode",
    )(page_table, cache_ckv, cache_krope, q_s, c_new, kr_new, wukt, wuv, gkn)


GROWS = G_B * DEC_SEQ
FOX_PAGES = 64
FOX_KEYS = FOX_PAGES * PAGE_SIZE
FOX_CHUNKS = N_PAGES // FOX_PAGES


def _fox_decode_body(pt_ref, k_hbm, v_hbm, q_ref, kn_ref, vn_ref, fq_ref, fk_ref, fn_ref, o_ref,
                     kbuf, vbuf, ksem, vsem, kn_pad, vn_pad, q_scr, m_scr, l_scr, acc_scr, *, layer):
    b, c = pl.program_id(0), pl.program_id(1)

    @pl.when((b == 0) & (c == 0))
    def _():
        kn_pad[...] = jnp.zeros(kn_pad.shape, f32)
        vn_pad[...] = jnp.zeros(vn_pad.shape, f32)

    gather = (pt_ref, layer, FOX_CHUNKS, FOX_PAGES,
              [(k_hbm, kbuf, ksem, PAGE_SIZE * KV_B), (v_hbm, vbuf, vsem, PAGE_SIZE * KV_B)])
    _gather_prefetch(*gather)
    slot = _gather_wait(*gather)

    @pl.when(c == 0)
    def _():
        q = q_ref[...]
        q_scr[...] = jnp.concatenate([q[:, h * HD_B:(h + 1) * HD_B] for h in range(H_B)], axis=0).astype(bf16)
        m_scr[...] = jnp.full(m_scr.shape, NEG_INF, f32)
        l_scr[...] = jnp.zeros(l_scr.shape, f32)
        acc_scr[...] = jnp.zeros(acc_scr.shape, f32)

    def bias(fk):
        rows = [fq_ref[h * DEC_SEQ:(h + 1) * DEC_SEQ, :] - fk[h:h + 1, :] for h in range(H_B)]
        return jnp.concatenate(rows, axis=0)

    def attend(kv, k_bf, v_bf, bias_kv, mask):
        rows = slice(kv * GROWS, (kv + 1) * GROWS)
        s = lax.dot_general(q_scr[rows, :], k_bf, NT, preferred_element_type=f32) + bias_kv
        if mask is not None:
            s = jnp.where(mask, s, NEG_INF)
        _softmax_update(s, v_bf, m_scr.at[rows], l_scr.at[rows], acc_scr.at[rows])

    bias_past = bias(fk_ref[...])
    for kv in range(KV_B):
        sel = pl.ds(kv, FOX_KEYS, stride=KV_B)
        attend(kv, kbuf[slot, sel, :].astype(bf16), vbuf[slot, sel, :].astype(bf16),
               bias_past[kv * GROWS:(kv + 1) * GROWS], None)

    @pl.when(c == FOX_CHUNKS - 1)
    def _():
        bias_new = bias(fn_ref[...])
        mask = _new_key_mask(GROWS)
        for kv in range(KV_B):
            kn_pad[0:DEC_SEQ, :] = kn_ref[:, kv * HD_B:(kv + 1) * HD_B]
            vn_pad[0:DEC_SEQ, :] = vn_ref[:, kv * HD_B:(kv + 1) * HD_B]
            attend(kv, kn_pad[...].astype(bf16), vn_pad[...].astype(bf16),
                   bias_new[kv * GROWS:(kv + 1) * GROWS], mask)
        o = acc_scr[...] / l_scr[...]
        for h in range(H_B):
            o_ref[:, h * HD_B:(h + 1) * HD_B] = o[h * DEC_SEQ:(h + 1) * DEC_SEQ, :].astype(o_ref.dtype)

    _gather_drain(*gather)


def _fox_decode(page_table, cache_k, cache_v, q_s, k_new, v_new, fq_col, fk_past, fn_new, layer):
    seq = lambda w: pl.BlockSpec((None, DEC_SEQ, w), lambda b, c, pt: (b, 0, 0))
    grid_spec = pltpu.PrefetchScalarGridSpec(
        num_scalar_prefetch=1,
        grid=(DEC_BATCH, FOX_CHUNKS),
        in_specs=[pl.BlockSpec(memory_space=pl.ANY), pl.BlockSpec(memory_space=pl.ANY),
                  seq(H_B * HD_B), seq(KV_B * HD_B), seq(KV_B * HD_B),
                  pl.BlockSpec((None, H_B * DEC_SEQ, 1), lambda b, c, pt: (b, 0, 0)),
                  pl.BlockSpec((None, H_B, FOX_KEYS), lambda b, c, pt: (b, 0, c)),
                  pl.BlockSpec((None, H_B, LANES), lambda b, c, pt: (b, 0, 0))],
        out_specs=seq(H_B * HD_B),
        scratch_shapes=[
            pltpu.VMEM((2, FOX_KEYS * KV_B, HD_B), f32), pltpu.VMEM((2, FOX_KEYS * KV_B, HD_B), f32),
            pltpu.SemaphoreType.DMA((2,)), pltpu.SemaphoreType.DMA((2,)),
            pltpu.VMEM((LANES, HD_B), f32), pltpu.VMEM((LANES, HD_B), f32),
            pltpu.VMEM((H_B * DEC_SEQ, HD_B), bf16),
            pltpu.VMEM((H_B * DEC_SEQ, 1), f32), pltpu.VMEM((H_B * DEC_SEQ, 1), f32),
            pltpu.VMEM((H_B * DEC_SEQ, HD_B), f32),
        ],
    )
    return pl.pallas_call(
        functools.partial(_fox_decode_body, layer=layer),
        grid_spec=grid_spec,
        out_shape=jax.ShapeDtypeStruct((DEC_BATCH, DEC_SEQ, H_B * HD_B), f32),
        compiler_params=_cparams(("arbitrary", "arbitrary")),
        name="fox_decode",
    )(page_table, cache_k, cache_v, q_s, k_new, v_new, fq_col, fk_past, fn_new)


C_TM = 256
C_IN = (H_C + 2 * KV_C) * HD_C


def _norm_pair_tile(t, g128, lo):
    ss = t * t
    s_lo = jnp.sum(jnp.where(lo, ss, 0.0), axis=-1, keepdims=True)
    s_hi = jnp.sum(jnp.where(lo, 0.0, ss), axis=-1, keepdims=True)
    r = jnp.where(lo, lax.rsqrt(s_lo / 64.0 + EPS), lax.rsqrt(s_hi / 64.0 + EPS))
    return (t * r) * g128


def _cproj_body(x_ref, mg_ref, win_ref, gq_ref, gk_ref, q_ref, k_ref, v_ref):
    n = _rms(x_ref[...], mg_ref[...]).astype(bf16)
    y = jnp.dot(n, win_ref[...], preferred_element_type=f32)
    lo, _ = _half_masks((C_TM, LANES))
    nq = H_C * HD_C
    for p in range(nq // LANES):
        q_ref[:, p * LANES:(p + 1) * LANES] = (
            _norm_pair_tile(y[:, p * LANES:(p + 1) * LANES], gq_ref[...], lo) * C_SCALE).astype(bf16)
    for p in range(KV_C * HD_C // LANES):
        k_ref[:, p * LANES:(p + 1) * LANES] = _norm_pair_tile(y[:, nq + p * LANES:nq + (p + 1) * LANES], gk_ref[...], lo)
    v_ref[...] = y[:, nq + KV_C * HD_C:]


def _cproj(x, mg, win, gq128, gk128):
    m = x.shape[0]
    row = lambda w: pl.BlockSpec((C_TM, w), lambda i: (i, 0))
    full = lambda a: pl.BlockSpec(a.shape, lambda i: (0,) * a.ndim)
    return pl.pallas_call(
        _cproj_body,
        grid=(m // C_TM,),
        in_specs=[row(D_MODEL), full(mg), full(win), full(gq128), full(gk128)],
        out_specs=[row(H_C * HD_C), row(KV_C * HD_C), row(KV_C * HD_C)],
        out_shape=[jax.ShapeDtypeStruct((m, H_C * HD_C), bf16), jax.ShapeDtypeStruct((m, KV_C * HD_C), f32),
                   jax.ShapeDtypeStruct((m, KV_C * HD_C), f32)],
        compiler_params=_cparams(("parallel",)),
        name="swa_proj",
    )(x, mg, win, gq128, gk128)


def _t5_bucket_np(dist):
    n = np.maximum(dist, 0)
    nf = np.maximum(n, 1).astype(np.float32)
    large = MAX_EXACT + (np.log(nf / np.float32(MAX_EXACT)) / np.float32(math.log(MAX_DISTANCE / MAX_EXACT))
                         * np.float32(NUM_BUCKETS - MAX_EXACT)).astype(np.int32)
    return np.where(n < MAX_EXACT, n, np.minimum(large, NUM_BUCKETS - 1)).astype(np.int32)


def _window_bucket_and_mask(q_loc, n_keys):
    dist = q_loc[:, None] - np.arange(n_keys)[None, :]
    return np.where((dist >= 0) & (dist < WINDOW), _t5_bucket_np(dist), -1).astype(np.int32)


def _fill_bias(rel_ref, bucket, bias_scr):
    def one_head(h, carry):
        acc = jnp.full(bucket.shape, NEG_INF, f32)
        for bk in range(NUM_BUCKETS):
            acc = jnp.where(bucket == bk, rel_ref[bk, h], acc)
        bias_scr[h] = acc
        return carry
    lax.fori_loop(0, H_C, one_head, 0)


def _sink_softmax_pv(s, sink, v_bf):
    m = jnp.maximum(jnp.max(s, axis=-1, keepdims=True), sink)
    e = jnp.exp(s - m)
    den = jnp.sum(e, axis=-1, keepdims=True) + jnp.exp(sink - m)
    return jnp.dot(e.astype(bf16), v_bf, preferred_element_type=f32) / den


def _swa_prompt_body(rel_ref, sink_ref, bucket_ref, q_ref, kp_ref, kc_ref, vp_ref, vc_ref, o_ref, bias_scr):
    b, n = pl.program_id(0), pl.program_id(1)

    @pl.when((b == 0) & (n == 0))
    def _():
        _fill_bias(rel_ref, bucket_ref[...], bias_scr)

    def block(first):
        kb = jnp.concatenate([kp_ref[...], kc_ref[...]], axis=0).astype(bf16)
        vb = jnp.concatenate([vp_ref[...], vc_ref[...]], axis=0).astype(bf16)
        if first:
            has_prev = lax.broadcasted_iota(i32, (WINDOW, 2 * WINDOW), 1) >= WINDOW
        for kv in range(KV_C):
            k_h = kb[:, kv * HD_C:(kv + 1) * HD_C]
            v_h = vb[:, kv * HD_C:(kv + 1) * HD_C]
            outs = []
            for g in range(G_C):
                h = kv * G_C + g
                s = lax.dot_general(q_ref[:, h * HD_C:(h + 1) * HD_C], k_h, NT,
                                    preferred_element_type=f32) + bias_scr[h]
                if first:
                    s = jnp.where(has_prev, s, NEG_INF)
                outs.append(_sink_softmax_pv(s, sink_ref[h], v_h))
            for p in range(G_C // 2):
                tile = jnp.concatenate([outs[2 * p], outs[2 * p + 1]], axis=1)
                col0 = (kv * G_C + 2 * p) * HD_C
                o_ref[:, col0:col0 + LANES] = tile.astype(o_ref.dtype)

    @pl.when(n == 0)
    def _():
        block(True)

    @pl.when(n > 0)
    def _():
        block(False)


def _swa_prompt(q, k, v, rel_bias, sinks):
    nb = SEQ // WINDOW
    bucket = jnp.asarray(_window_bucket_and_mask(WINDOW + np.arange(WINDOW), 2 * WINDOW))
    smem = pl.BlockSpec(memory_space=pltpu.SMEM)
    cur = lambda w: pl.BlockSpec((WINDOW, w), lambda b, n: (b * nb + n, 0))
    prev = lambda w: pl.BlockSpec((WINDOW, w), lambda b, n: (b * nb + jnp.maximum(n - 1, 0), 0))
    kvw = KV_C * HD_C
    return pl.pallas_call(
        _swa_prompt_body,
        grid=(BATCH, nb),
        in_specs=[smem, smem, pl.BlockSpec(bucket.shape, lambda b, n: (0, 0)),
                  cur(H_C * HD_C), prev(kvw), cur(kvw), prev(kvw), cur(kvw)],
        out_specs=cur(H_C * HD_C),
        out_shape=jax.ShapeDtypeStruct((MP, H_C * HD_C), bf16),
        scratch_shapes=[pltpu.VMEM((H_C, WINDOW, 2 * WINDOW), f32)],
        compiler_params=_cparams(("arbitrary", "arbitrary")),
        name="swa_prompt",
    )(rel_bias, sinks, bucket, q, k, k, v, v)


SWA_SB = 4


def _swa_decode_body(rel_ref, sink_ref, bucket_ref, q_ref, kb_ref, vb_ref, kn_ref, vn_ref,
                     o_ref, ko_ref, vo_ref, bias_scr, kn_pad, vn_pad):
    b = pl.program_id(0)

    @pl.when(b == 0)
    def _():
        _fill_bias(rel_ref, bucket_ref[...], bias_scr)
        kn_pad[...] = jnp.zeros(kn_pad.shape, f32)
        vn_pad[...] = jnp.zeros(vn_pad.shape, f32)

    keep = WINDOW - DEC_SEQ
    for sl in range(SWA_SB):
        kn_pad[sl, 0:DEC_SEQ, :] = kn_ref[sl]
        vn_pad[sl, 0:DEC_SEQ, :] = vn_ref[sl]
        keys = jnp.concatenate([kb_ref[sl], kn_pad[sl]], axis=0).astype(bf16)
        vals = jnp.concatenate([vb_ref[sl], vn_pad[sl]], axis=0).astype(bf16)
        q = q_ref[sl]
        for kv in range(KV_C):
            k_h = keys[:, kv * HD_C:(kv + 1) * HD_C]
            v_h = vals[:, kv * HD_C:(kv + 1) * HD_C]
            q_kv = jnp.concatenate([q[:, (kv * G_C + g) * HD_C:(kv * G_C + g + 1) * HD_C] for g in range(G_C)],
                                   axis=0).astype(bf16)
            s_all = lax.dot_general(q_kv, k_h, NT, preferred_element_type=f32)
            outs = []
            for g in range(G_C):
                h = kv * G_C + g
                s = s_all[g * DEC_SEQ:(g + 1) * DEC_SEQ, :] + bias_scr[h]
                outs.append(_sink_softmax_pv(s, sink_ref[h], v_h))
            for p in range(G_C // 2):
                tile = jnp.concatenate([outs[2 * p], outs[2 * p + 1]], axis=1)
                col0 = (kv * G_C + 2 * p) * HD_C
                o_ref[sl, :, col0:col0 + LANES] = tile.astype(o_ref.dtype)
        ko_ref[sl, 0:keep, :] = kb_ref[sl, DEC_SEQ:WINDOW, :]
        ko_ref[sl, keep:WINDOW, :] = kn_ref[sl]
        vo_ref[sl, 0:keep, :] = vb_ref[sl, DEC_SEQ:WINDOW, :]
        vo_ref[sl, keep:WINDOW, :] = vn_ref[sl]


def _swa_decode(q_s, buf_k, buf_v, k_new, v_new, rel_bias, sinks):
    bucket = jnp.asarray(_window_bucket_and_mask(WINDOW + np.arange(DEC_SEQ), 2 * WINDOW))
    smem = pl.BlockSpec(memory_space=pltpu.SMEM)
    kvw = KV_C * HD_C
    seq = lambda r, w: pl.BlockSpec((SWA_SB, r, w), lambda b: (b, 0, 0))
    return pl.pallas_call(
        _swa_decode_body,
        grid=(DEC_BATCH // SWA_SB,),
        in_specs=[smem, smem, pl.BlockSpec(bucket.shape, lambda b: (0, 0)),
                  seq(DEC_SEQ, H_C * HD_C), seq(WINDOW, kvw), seq(WINDOW, kvw), seq(DEC_SEQ, kvw), seq(DEC_SEQ, kvw)],
        out_specs=[seq(DEC_SEQ, H_C * HD_C), seq(WINDOW, kvw), seq(WINDOW, kvw)],
        out_shape=[jax.ShapeDtypeStruct((DEC_BATCH, DEC_SEQ, H_C * HD_C), f32),
                   jax.ShapeDtypeStruct((DEC_BATCH, WINDOW, kvw), f32),
                   jax.ShapeDtypeStruct((DEC_BATCH, WINDOW, kvw), f32)],
        scratch_shapes=[pltpu.VMEM((H_C, DEC_SEQ, 2 * WINDOW), f32), pltpu.VMEM((SWA_SB, WINDOW, kvw), f32),
                        pltpu.VMEM((SWA_SB, WINDOW, kvw), f32)],
        compiler_params=_cparams(("arbitrary",)),
        name="swa_decode",
    )(rel_bias, sinks, bucket, q_s, buf_k, buf_v, k_new, v_new)


def _rope_tables():
    half = ROPE_D // 2
    inv = ROPE_THETA ** (-jnp.arange(half, dtype=f32) / half)
    pos = jnp.concatenate([jnp.tile(jnp.arange(SEQ, dtype=i32), BATCH),
                           jnp.tile(PAST_LEN + jnp.arange(DEC_SEQ, dtype=i32), DEC_BATCH)])
    ang = pos.astype(f32)[:, None] * inv[None, :]
    cos, sin = jnp.cos(ang), jnp.sin(ang)
    return jnp.tile(cos, (1, 4)), jnp.tile(jnp.concatenate([-sin, sin], axis=1), (1, 2))


def _row(v, reps=1):
    return jnp.tile(v.astype(f32), reps).reshape(1, -1)


def _ab_layer(x, j, tabs, caches, page_table, p):
    cos128, sin128 = tabs
    cache_ckv, cache_krope, cache_k, cache_v, cache_logf = caches
    w_in = p["ab_w_in"][j]
    pad = jnp.zeros((D_MODEL, IN_AB_PAD - w_in.shape[1]), f32)
    win = jnp.concatenate([w_in[:, 0:1024], w_in[:, 1088:2624], w_in[:, 1024:1088], w_in[:, 2624:2632], pad],
                          axis=1).astype(bf16)
    wq = p["mla_w_q_up"][j].reshape(Q_LORA, H_A, NOPE_D + ROPE_D)
    wqu = jnp.concatenate([wq[:, :, :NOPE_D].reshape(Q_LORA, -1), wq[:, :, NOPE_D:].reshape(Q_LORA, -1)],
                          axis=1).astype(bf16)
    bf128 = jnp.zeros((1, LANES), f32).at[0, 64:64 + H_B].set(p["fox_b_f"][j])
    gkr128 = jnp.concatenate([p["mla_g_kr"][j], jnp.ones((64,), f32)]).reshape(1, LANES)
    qmla, ckv, krope, kr128, fq, fk, fv, logf = _abproj(
        x, _row(p["mix_g"][2 * j]), win, _row(p["mla_g_q_lat"][j]), wqu, _row(p["mla_g_kv_lat"][j]),
        _row(p["mla_g_qn"][j]), _row(p["mla_g_qr"][j], 2), gkr128, _row(p["fox_g_q"][j]), _row(p["fox_g_k"][j]),
        bf128, cos128, sin128)
    wuk = p["mla_w_uk"][j].astype(bf16)
    wuv = p["mla_w_uv"][j].astype(bf16)
    gkn = _row(p["mla_g_kn"][j])
    proj = (qmla, ckv, krope, kr128, fq, fk, fv, logf)
    o_p = _ab_prompt_attn(proj, wuk, wuv, gkn)
    o_s = _ab_sample_attn(proj, caches, page_table, wuk, wuv, gkn, j)
    x = _outproj(x, jnp.concatenate([o_p, o_s], axis=0), p["ab_w_out"][j].astype(bf16))
    seqs = lambda a: a[MP:].reshape(DEC_BATCH, DEC_SEQ, -1)
    st_p = (ckv[:MP].reshape(BATCH, SEQ, KV_LORA), krope[:MP].reshape(BATCH, SEQ, ROPE_D),
            fk[:MP].reshape(BATCH, SEQ, KV_B, HD_B), fv[:MP].reshape(BATCH, SEQ, KV_B, HD_B),
            logf[:MP].reshape(BATCH, SEQ, H_B))
    st_s = (seqs(ckv), seqs(krope), seqs(fk).reshape(DEC_BATCH, DEC_SEQ, KV_B, HD_B),
            seqs(fv).reshape(DEC_BATCH, DEC_SEQ, KV_B, HD_B), seqs(logf))
    return x, st_p, st_s


def _ab_prompt_attn(proj, wuk, wuv, gkn):
    qmla, ckv, krope, kr128, fq, fk, fv, logf = proj
    kmla, va = _kvup(ckv, kr128, wuk, wuv, gkn)
    o_a = _flash_prompt(qmla, kmla, va, H_A, 2, 1, QH, V_D)
    lf_p = logf[:MP].reshape(BATCH, SEQ, H_B)
    f_t = _cumsum_prompt(jnp.swapaxes(lf_p, 1, 2))
    f_tok = jnp.swapaxes(f_t, 1, 2).reshape(MP, H_B)
    o_b = _flash_prompt(fq, fk, fv, KV_B, KV_B, G_B, HD_B, HD_B, fq=f_tok, fk=f_t)
    return jnp.concatenate([o_a, o_b], axis=1)


def _ab_sample_attn(proj, caches, page_table, wuk, wuv, gkn, j):
    qmla, ckv, krope, kr128, fq, fk, fv, logf = proj
    cache_ckv, cache_krope_t, cache_k, cache_v, cache_logf_t = caches
    seqs = lambda a: a[MP:].reshape(DEC_BATCH, DEC_SEQ, -1)
    lf_new_t = jnp.swapaxes(seqs(logf), 1, 2).reshape(DEC_BATCH * H_B, DEC_SEQ)
    lf_new_t = jnp.pad(lf_new_t, ((0, 0), (0, CS_CHUNK - DEC_SEQ)))
    fk_past, fn_new = _lf_sample(page_table, cache_logf_t, lf_new_t, j)
    fk_past = fk_past.reshape(DEC_BATCH, H_B, PAST_LEN)
    fn_new = fn_new.reshape(DEC_BATCH, H_B, LANES)
    fq_col = fn_new[:, :, :DEC_SEQ].reshape(DEC_BATCH, H_B * DEC_SEQ, 1)
    os_a = _mla_decode(page_table, cache_ckv, cache_krope_t, seqs(qmla).astype(f32), seqs(ckv), seqs(krope),
                       wuk.T, wuv, gkn, j)
    os_b = _fox_decode(page_table, cache_k, cache_v, seqs(fq).astype(f32), seqs(fk), seqs(fv),
                       fq_col, fk_past, fn_new, j)
    return jnp.concatenate([os_a, os_b], axis=2).reshape(MS, D_MODEL).astype(bf16)


def _c_layer(x, j, state_k, state_v, p):
    kvw = KV_C * HD_C
    q, k, v = _cproj(x, _row(p["mix_g"][2 * j + 1]), p["swa_w_in"][j].astype(bf16),
                     _row(p["swa_g_q"][j], 2), _row(p["swa_g_k"][j], 2))
    rel = p["rel_bias"].astype(f32)
    sinks = p["swa_sinks"][j].astype(f32)
    o_p = _swa_prompt(q, k, v, rel, sinks)
    seqs = lambda a: a[MP:].reshape(DEC_BATCH, DEC_SEQ, -1)
    o_s, k_out, v_out = _swa_decode(seqs(q).astype(f32), state_k[j].reshape(DEC_BATCH, WINDOW, kvw),
                                    state_v[j].reshape(DEC_BATCH, WINDOW, kvw), seqs(k), seqs(v), rel, sinks)
    o = jnp.concatenate([o_p, o_s.reshape(MS, D_MODEL).astype(bf16)], axis=0)
    x = _outproj(x, o, p["swa_w_out"][j].astype(bf16))
    tail = lambda a: a[:MP].reshape(BATCH, SEQ, KV_C, HD_C)[:, SEQ - WINDOW:]
    st_p = (tail(k), tail(v))
    st_s = (k_out.reshape(DEC_BATCH, WINDOW, KV_C, HD_C), v_out.reshape(DEC_BATCH, WINDOW, KV_C, HD_C))
    return x, st_p, st_s


def kernel(x_prompt, x_sample, cache_mla_ckv, cache_mla_krope, cache_fox_k, cache_fox_v, cache_fox_logf, state_swa_k, state_swa_v, page_table, ffn1_g, ffn1_w_gate, ffn1_w_up, ffn1_w_down, mix_g, ffn2_g, ffn2_w_gate, ffn2_w_up, ffn2_w_down, ab_w_in, mla_g_q_lat, mla_w_q_up, mla_g_kv_lat, mla_w_uk, mla_w_uv, mla_g_qn, mla_g_qr, mla_g_kn, mla_g_kr, fox_g_q, fox_g_k, fox_b_f, ab_w_out, swa_w_in, swa_g_q, swa_g_k, swa_sinks, swa_w_out, rel_bias):
    assert WINDOW == PAGE_SIZE == LANES and min(WINDOW, PAST_LEN) == WINDOW
    p = dict(mix_g=mix_g, ab_w_in=ab_w_in, mla_g_q_lat=mla_g_q_lat, mla_w_q_up=mla_w_q_up,
             mla_g_kv_lat=mla_g_kv_lat, mla_w_uk=mla_w_uk, mla_w_uv=mla_w_uv, mla_g_qn=mla_g_qn,
             mla_g_qr=mla_g_qr, mla_g_kn=mla_g_kn, mla_g_kr=mla_g_kr, fox_g_q=fox_g_q, fox_g_k=fox_g_k,
             fox_b_f=fox_b_f, ab_w_out=ab_w_out, swa_w_in=swa_w_in, swa_g_q=swa_g_q, swa_g_k=swa_g_k,
             swa_sinks=swa_sinks, swa_w_out=swa_w_out, rel_bias=rel_bias)
    n_pool = cache_fox_k.shape[1]
    caches = (cache_mla_ckv, jnp.swapaxes(cache_mla_krope, 2, 3),
              cache_fox_k.reshape(-1, n_pool, PAGE_SIZE * KV_B, HD_B),
              cache_fox_v.reshape(-1, n_pool, PAGE_SIZE * KV_B, HD_B), jnp.swapaxes(cache_fox_logf, 2, 3))
    tabs = _rope_tables()
    x = jnp.concatenate([x_prompt.reshape(MP, D_MODEL), x_sample.reshape(MS, D_MODEL)], axis=0)
    ab_p, ab_s, c_p, c_s = [], [], [], []
    for l in range(DEPTH):
        x = _ffn(x, ffn1_g[l], ffn1_w_gate, ffn1_w_up, ffn1_w_down, l)
        j = l // 2
        if l % 2 == 0:
            x, st_p, st_s = _ab_layer(x, j, tabs, caches, page_table, p)
            ab_p.append(st_p)
            ab_s.append(st_s)
        else:
            x, st_p, st_s = _c_layer(x, j, state_swa_k, state_swa_v, p)
            c_p.append(st_p)
            c_s.append(st_s)
        x = _ffn(x, ffn2_g[l], ffn2_w_gate, ffn2_w_up, ffn2_w_down, l)

    def stack(states, i):
        return jnp.stack([st[i] for st in states], axis=0)

    return (x[:MP].reshape(BATCH, SEQ, D_MODEL), x[MP:].reshape(DEC_BATCH, DEC_SEQ, D_MODEL),
            stack(ab_p, 0), stack(ab_p, 1), stack(ab_p, 2), stack(ab_p, 3), stack(ab_p, 4),
            stack(c_p, 0), stack(c_p, 1),
            stack(ab_s, 0), stack(ab_s, 1), stack(ab_s, 2), stack(ab_s, 3), stack(ab_s, 4),
            stack(c_s, 0), stack(c_s, 1))
```
